```python
import math
import jax
import jax.numpy as jnp
from jax import lax
import numpy as np

D_MODEL = 1024
BATCH = 8
SEQ = 4096
DEPTH = 2

GRID_W = 64
CTX_LEN = 256
HEAD_DIM = 64
NA_HEADS = 6
NA_WIN_R = 8
NA_WIN_C = 16
NA_QCB = 16
NA_KCB = 32
WG_HEADS = 6
WG_KV_HEADS = 2
WG_WINDOW = 128
WG_BLOCK = 128
SC_CH = 256
SC_GROUPS = 4
CONV_W = 3
D_FF = 2816
ROPE_BASE = 10000.0
EPS = 1e-6
NEG = -1e30

NA_W = NA_HEADS * HEAD_DIM
WG_W = WG_HEADS * HEAD_DIM
WG_KV_W = WG_KV_HEADS * HEAD_DIM
MIX_W = NA_W + WG_W + SC_CH
IN_SPLITS = (NA_W, NA_W, NA_W, WG_W, WG_KV_W, WG_KV_W, SC_CH, SC_CH, SC_CH)
IN_W = sum(IN_SPLITS)

kernel_name = 'hybrid_natten_swa_shortconv_dit_block'


def _rmsnorm(x, g):
    xf = x.astype(jnp.float32)
    y = xf * lax.rsqrt(jnp.mean(xf * xf, axis=-1, keepdims=True) + EPS)
    return (y * g.astype(jnp.float32)).astype(x.dtype)


def _modulate(h, shift, scale):
    return h * (1 + scale) + shift


def _dwconv(x, w):
    L = x.shape[1]
    pad = CONV_W // 2
    xp = jnp.pad(x, ((0, 0), (pad, CONV_W - 1 - pad), (0, 0)))
    out = xp[:, 0:L] * w[0]
    for k in range(1, CONV_W):
        out = out + xp[:, k:k + L] * w[k]
    return out


def _axial_rope(x):
    L = x.shape[1]
    t = jnp.arange(L)
    row = (t // GRID_W).astype(jnp.float32)
    col = (t % GRID_W).astype(jnp.float32)
    half = HEAD_DIM // 2
    n_freq = half // 2
    inv = ROPE_BASE ** (-jnp.arange(n_freq, dtype=jnp.float32) / n_freq)
    ang = jnp.concatenate([row[:, None] * inv, col[:, None] * inv], axis=-1)
    cos = jnp.cos(ang)[None, :, None, :]
    sin = jnp.sin(ang)[None, :, None, :]
    xf = x.astype(jnp.float32)
    x1, x2 = xf[..., :half], xf[..., half:]
    return jnp.concatenate([x1 * cos - x2 * sin, x2 * cos + x1 * sin], axis=-1).astype(x.dtype)


def _ctx_attention(q, k, v, sink=None):
    B, C, Hq, dh = q.shape
    Hkv = k.shape[2]
    G = Hq // Hkv
    qg = q.reshape(B, C, Hkv, G, dh)
    s = jnp.einsum('bqkgd,bckd->bkgqc', qg, k).astype(jnp.float32) * (1.0 / math.sqrt(dh))
    if sink is not None:
        s_sink = jnp.broadcast_to(sink.astype(jnp.float32).reshape(1, Hkv, G, 1, 1), s.shape[:-1] + (1,))
        s = jnp.concatenate([s, s_sink], axis=-1)
    p = jax.nn.softmax(s, axis=-1)[..., :C].astype(v.dtype)
    o = jnp.einsum('bkgqc,bckd->bqkgd', p, v)
    return o.reshape(B, C, Hq * dh)


def _neighbourhood_attention(q, k, v, kc, vc, rpb):
    B, S, H, dh = q.shape
    rows = S // GRID_W
    wr = min(NA_WIN_R, rows)
    ncb = GRID_W // NA_QCB
    nk = wr * NA_KCB
    scale = 1.0 / math.sqrt(dh)
    qg = q.reshape(B, rows, GRID_W, H, dh)
    kg = k.reshape(B, rows, GRID_W, H, dh)
    vg = v.reshape(B, rows, GRID_W, H, dh)
    qcols = np.arange(GRID_W).reshape(ncb, NA_QCB)
    c0 = np.clip(qcols - NA_WIN_C // 2, 0, GRID_W - NA_WIN_C)
    kcols = np.clip(np.arange(ncb) * NA_QCB - NA_WIN_C // 2, 0, GRID_W - NA_KCB)[:, None] + np.arange(NA_KCB)
    col_ok = (kcols[:, None, :] >= c0[:, :, None]) & (kcols[:, None, :] < c0[:, :, None] + NA_WIN_C)
    mask = np.broadcast_to(col_ok[:, :, None, :], (ncb, NA_QCB, wr, NA_KCB)).reshape(ncb, NA_QCB, nk)
    dcol = np.clip(kcols[:, None, :] - qcols[:, :, None] + NA_WIN_C - 1, 0, 2 * NA_WIN_C - 2)

    def row_block(r):
        r0 = jnp.clip(r - wr // 2, 0, rows - wr)
        kr = lax.dynamic_slice_in_dim(kg, r0, wr, axis=1)[:, :, kcols]
        vr = lax.dynamic_slice_in_dim(vg, r0, wr, axis=1)[:, :, kcols]
        kr = jnp.transpose(kr, (0, 2, 1, 3, 4, 5)).reshape(B, ncb, nk, H, dh)
        vr = jnp.transpose(vr, (0, 2, 1, 3, 4, 5)).reshape(B, ncb, nk, H, dh)
        qr = lax.dynamic_index_in_dim(qg, r, axis=1, keepdims=False).reshape(B, ncb, NA_QCB, H, dh)
        drow = r0 + jnp.arange(wr) - r + NA_WIN_R - 1
        bias = rpb[:, drow[None, None, :, None], dcol[:, :, None, :]].reshape(H, ncb, NA_QCB, nk).astype(jnp.float32)
        s_loc = jnp.einsum('bjqhd,bjkhd->bhjqk', qr, kr).astype(jnp.float32) * scale + bias
        s_loc = jnp.where(mask, s_loc, NEG)
        s_ctx = jnp.einsum('bjqhd,bchd->bhjqc', qr, kc).astype(jnp.float32) * scale
        p = jax.nn.softmax(jnp.concatenate([s_loc, s_ctx], axis=-1), axis=-1).astype(v.dtype)
        o = (jnp.einsum('bhjqk,bjkhd->bjqhd', p[..., :nk], vr)
             + jnp.einsum('bhjqc,bchd->bjqhd', p[..., nk:], vc))
        return o.reshape(B, GRID_W, H * dh)

    out = lax.map(row_block, jnp.arange(rows))
    return jnp.transpose(out, (1, 0, 2, 3)).reshape(B, S, H * dh)


def _window_gqa(q, k, v, kc, vc, sink):
    B, S, Hq, dh = q.shape
    Hkv = k.shape[2]
    G = Hq // Hkv
    C = kc.shape[1]
    nb = S // WG_BLOCK
    span = 3 * WG_BLOCK
    scale = 1.0 / math.sqrt(dh)
    qb = q.reshape(B, nb, WG_BLOCK, Hkv, G, dh)
    pad = ((0, 0), (WG_BLOCK, WG_BLOCK), (0, 0), (0, 0))
    kp = jnp.pad(k, pad)
    vp = jnp.pad(v, pad)
    s_sink = jnp.broadcast_to(sink.astype(jnp.float32).reshape(1, Hkv, G, 1, 1), (B, Hkv, G, WG_BLOCK, 1))

    def band_block(i):
        qi = lax.dynamic_index_in_dim(qb, i, axis=1, keepdims=False)
        ki = lax.dynamic_slice_in_dim(kp, i * WG_BLOCK, span, axis=1)
        vi = lax.dynamic_slice_in_dim(vp, i * WG_BLOCK, span, axis=1)
        qpos = i * WG_BLOCK + jnp.arange(WG_BLOCK)
        kpos = (i - 1) * WG_BLOCK + jnp.arange(span)
        ok = (jnp.abs(qpos[:, None] - kpos[None, :]) <= WG_WINDOW) & (kpos >= 0)[None, :] & (kpos < S)[None, :]
        s_loc = jnp.einsum('bqkgd,bnkd->bkgqn', qi, ki).astype(jnp.float32) * scale
        s_loc = jnp.where(ok, s_loc, NEG)
        s_ctx = jnp.einsum('bqkgd,bckd->bkgqc', qi, kc).astype(jnp.float32) * scale
        p = jax.nn.softmax(jnp.concatenate([s_loc, s_ctx, s_sink], axis=-1), axis=-1).astype(v.dtype)
        o = (jnp.einsum('bkgqn,bnkd->bqkgd', p[..., :span], vi)
             + jnp.einsum('bkgqc,bckd->bqkgd', p[..., span:span + C], vc))
        return o.reshape(B, WG_BLOCK, Hq * dh)

    out = lax.map(band_block, jnp.arange(nb))
    return jnp.transpose(out, (1, 0, 2, 3)).reshape(B, S, Hq * dh)


def _mixer_inputs(h, w_in, qn_a, kn_a, qn_b, kn_b):
    B, L, _ = h.shape
    p = h @ w_in
    cuts = [int(i) for i in np.cumsum(IN_SPLITS)[:-1]]
    qa, ka, va, qb, kb, vb, u, bg, cg = jnp.split(p, cuts, axis=-1)
    hd = lambda t: t.reshape(B, L, -1, HEAD_DIM)
    qa = _rmsnorm(hd(qa), qn_a)
    ka = _rmsnorm(hd(ka), kn_a)
    qb = _rmsnorm(hd(qb), qn_b)
    kb = _rmsnorm(hd(kb), kn_b)
    return qa, ka, hd(va), qb, kb, hd(vb), u, bg, cg


def _conv_ffn(h, w_up, conv_ffn, w_down):
    a, g = jnp.split(_dwconv(h @ w_up, conv_ffn), 2, axis=-1)
    return (jax.nn.silu(a) * g) @ w_down


def _layer(xl, xc, mod_l, mod_c, g_attn, w_in, qn_a, kn_a, qn_b, kn_b, rpb_a, sink_b, conv_c, w_o,
           g_ffn, w_up, conv_ffn, w_down, update_ctx):
    sh1, sc1, gt1, sh2, sc2, gt2 = jnp.split(mod_l[:, None, :], 6, axis=-1)
    csh1, csc1, cgt1, csh2, csc2, cgt2 = jnp.split(mod_c, 6, axis=-1)
    hl = _modulate(_rmsnorm(xl, g_attn), sh1, sc1)
    hc = _modulate(_rmsnorm(xc, g_attn), csh1, csc1)
    qa, ka, va, qb, kb, vb, u, bg, cg = _mixer_inputs(hl, w_in, qn_a, kn_a, qn_b, kn_b)
    qa_c, ka_c, va_c, qb_c, kb_c, vb_c, u_c, bg_c, cg_c = _mixer_inputs(hc, w_in, qn_a, kn_a, qn_b, kn_b)
    qb = _axial_rope(qb)
    kb = _axial_rope(kb)
    ya = _neighbourhood_attention(qa, ka, va, ka_c, va_c, rpb_a)
    yb = _window_gqa(qb, kb, vb, kb_c, vb_c, sink_b)
    yc = bg * _dwconv(cg * u, conv_c)
    xl = xl + gt1 * (jnp.concatenate([ya, yb, yc], axis=-1) @ w_o)
    if update_ctx:
        ya_c = _ctx_attention(qa_c, ka_c, va_c)
        yb_c = _ctx_attention(qb_c, kb_c, vb_c, sink_b)
        yc_c = bg_c * _dwconv(cg_c * u_c, conv_c)
        xc = xc + cgt1 * (jnp.concatenate([ya_c, yb_c, yc_c], axis=-1) @ w_o)
    xl = xl + gt2 * _conv_ffn(_modulate(_rmsnorm(xl, g_ffn), sh2, sc2), w_up, conv_ffn, w_down)
    if update_ctx:
        xc = xc + cgt2 * _conv_ffn(_modulate(_rmsnorm(xc, g_ffn), csh2, csc2), w_up, conv_ffn, w_down)
    return xl, xc


def setup_inputs(seed: int = 0) -> dict:
    key = jax.random.key(seed)
    ks = jax.random.split(key, 20)
    nrm = lambda k, shape, s: jax.random.normal(k, shape, jnp.float32) * s
    return {
        'x': nrm(ks[0], (BATCH, SEQ, D_MODEL), 1.0),
        'c': nrm(ks[1], (BATCH, D_MODEL), 1.0),
        'ctx': nrm(ks[2], (BATCH, CTX_LEN, D_MODEL), 1.0),
        'c_ctx': nrm(ks[3], (D_MODEL,), 1.0),
        'w_ada': nrm(ks[4], (DEPTH, D_MODEL, 6 * D_MODEL), 0.5 * D_MODEL ** -0.5),
        'b_ada': nrm(ks[5], (DEPTH, 6 * D_MODEL), 0.02),
        'g_attn': 1.0 + nrm(ks[6], (DEPTH, D_MODEL), 0.01),
        'w_in': nrm(ks[7], (DEPTH, D_MODEL, IN_W), D_MODEL ** -0.5),
        'qn_a': 1.0 + nrm(ks[8], (DEPTH, HEAD_DIM), 0.01),
        'kn_a': 1.0 + nrm(ks[9], (DEPTH, HEAD_DIM), 0.01),
        'qn_b': 1.0 + nrm(ks[10], (DEPTH, HEAD_DIM), 0.01),
        'kn_b': 1.0 + nrm(ks[11], (DEPTH, HEAD_DIM), 0.01),
        'rpb_a': nrm(ks[12], (DEPTH, NA_HEADS, 2 * NA_WIN_R - 1, 2 * NA_WIN_C - 1), 0.1),
        'sink_b': nrm(ks[13], (DEPTH, WG_HEADS), 0.5),
        'conv_c': nrm(ks[14], (DEPTH, CONV_W, SC_CH), CONV_W ** -0.5),
        'w_o': nrm(ks[15], (DEPTH, MIX_W, D_MODEL), MIX_W ** -0.5),
        'g_ffn': 1.0 + nrm(ks[16], (DEPTH, D_MODEL), 0.01),
        'w_up': nrm(ks[17], (DEPTH, D_MODEL, 2 * D_FF), D_MODEL ** -0.5),
        'conv_ffn': nrm(ks[18], (DEPTH, CONV_W, 2 * D_FF), CONV_W ** -0.5),
        'w_down': nrm(ks[19], (DEPTH, D_FF, D_MODEL), D_FF ** -0.5),
    }


def reference(x, c, ctx, c_ctx, w_ada, b_ada, g_attn, w_in, qn_a, kn_a, qn_b, kn_b, rpb_a, sink_b,
              conv_c, w_o, g_ffn, w_up, conv_ffn, w_down):
    xl, xc = x, ctx
    sc = jax.nn.silu(c)
    scc = jax.nn.silu(c_ctx)
    for l in range(DEPTH):
        mod_l = sc @ w_ada[l] + b_ada[l]
        mod_c = scc @ w_ada[l] + b_ada[l]
        xl, xc = _layer(xl, xc, mod_l, mod_c, g_attn[l], w_in[l], qn_a[l], kn_a[l], qn_b[l], kn_b[l],
                        rpb_a[l], sink_b[l], conv_c[l], w_o[l], g_ffn[l], w_up[l], conv_ffn[l], w_down[l],
                        update_ctx=(l < DEPTH - 1))
    return xl
```

```python
import functools
import math

import jax
import jax.numpy as jnp
import numpy as np
from jax import lax
from jax.experimental import pallas as pl
from jax.experimental.pallas import tpu as pltpu

D_MODEL = 1024
DEPTH = 2
GRID_W = 64
HEAD_DIM = 64
NA_HEADS = 6
NA_WIN_R = 8
NA_WIN_C = 16
WG_HEADS = 6
WG_KV_HEADS = 2
WG_WINDOW = 128
WG_BLOCK = 128
SC_CH = 256
CONV_W = 3
D_FF = 2816
ROPE_BASE = 10000.0
EPS = 1e-6
NEG = -1e30

NA_W = NA_HEADS * HEAD_DIM
WG_W = WG_HEADS * HEAD_DIM
WG_KV_W = WG_KV_HEADS * HEAD_DIM
IN_W = 3 * NA_W + WG_W + 2 * WG_KV_W + 3 * SC_CH

LANES = 128
SUBLANES = 8
MXU_DIM = 256
FF_CHUNK = MXU_DIM
N_FF_CHUNKS = D_FF // FF_CHUNK
TOKEN_TILE = 512
NA_ROWS_PER_STEP = 4
VMEM_LIMIT = 48 * 1024 * 1024

_QA, _KA, _VA = 0, NA_W, 2 * NA_W
_QB = 3 * NA_W
_KB = _QB + WG_W
_VB = _KB + WG_KV_W
_U = _VB + WG_KV_W
_BG = _U + SC_CH
_CG = _BG + SC_CH

_F32 = jnp.float32
_BF16 = jnp.bfloat16


def _dot(a, b):
    return jnp.dot(a, b, preferred_element_type=_F32)


def _dot_nt(a, b):
    return lax.dot_general(a, b, (((1,), (1,)), ((), ())), preferred_element_type=_F32)


def _cparams(n_grid):
    return pltpu.CompilerParams(
        dimension_semantics=("arbitrary",) * n_grid, vmem_limit_bytes=VMEM_LIMIT)


def _ada_kernel(c_ref, w_ref, b_ref, o_ref):
    c = c_ref[...]
    s = c / (1.0 + jnp.exp(-c))
    s_hi = s.astype(_BF16)
    s_lo = (s - s_hi.astype(_F32)).astype(_BF16)
    w = w_ref[...]
    w_hi = w.astype(_BF16)
    w_lo = (w - w_hi.astype(_F32)).astype(_BF16)
    o_ref[...] = _dot(s_hi, w_hi) + _dot(s_lo, w_hi) + _dot(s_hi, w_lo) + b_ref[...]


def _ada_call(cvec, w_ada, b_ada):
    rows = cvec.shape[0]
    n_out = w_ada.shape[-1]
    nt = 1536
    return pl.pallas_call(
        _ada_kernel,
        grid=(DEPTH, n_out // nt),
        in_specs=[
            pl.BlockSpec((rows, D_MODEL), lambda l, j: (0, 0)),
            pl.BlockSpec((None, D_MODEL, nt), lambda l, j: (l, 0, j)),
            pl.BlockSpec((None, 1, nt), lambda l, j: (l, 0, j)),
        ],
        out_specs=pl.BlockSpec((None, rows, nt), lambda l, j: (l, 0, j)),
        out_shape=jax.ShapeDtypeStruct((DEPTH, rows, n_out), _F32),
        compiler_params=_cparams(2),
        name="adaln_mod",
    )(cvec, w_ada, b_ada.reshape(DEPTH, 1, n_out))


def _norm_mod(x, g, shift, scale):
    ms = jnp.mean(x * x, axis=-1, keepdims=True)
    return (x * lax.rsqrt(ms + EPS) * g) * (1.0 + scale) + shift


def _head_norm(x, e, gain):
    sq = x * x
    hi = sq.astype(_BF16)
    lo = (sq - hi.astype(_F32)).astype(_BF16)
    ms = _dot(hi, e) + _dot(lo, e)
    return x * lax.rsqrt(ms + EPS) * gain


def _lane_id(shape):
    return lax.broadcasted_iota(jnp.int32, shape, len(shape) - 1)


def _inproj_kernel(x_ref, mod_ref, g_ref, w_ref, gain_ref, e_ref, cos_ref, sin_ref,
                   qa_ref, ka_ref, va_ref, qb_ref, kb_ref, vb_ref, cu_ref, bg_ref):
    h = _norm_mod(x_ref[...], g_ref[...], mod_ref[0:1, :], mod_ref[1:2, :]).astype(_BF16)
    e = e_ref[...]

    for k in range(3):
        p = _dot(h, w_ref[:, _QA + k * MXU_DIM:_QA + (k + 1) * MXU_DIM])
        lo, hi = k * MXU_DIM, (k + 1) * MXU_DIM
        if hi <= NA_W:
            qa_ref[:, lo:hi] = _head_norm(p, e, gain_ref[0:1, :]).astype(_BF16)
        elif lo >= NA_W:
            ka_ref[:, lo - NA_W:hi - NA_W] = _head_norm(p, e, gain_ref[1:2, :]).astype(_BF16)
        else:
            gain = jnp.concatenate([gain_ref[0:1, :LANES], gain_ref[1:2, :LANES]], axis=1)
            pn = _head_norm(p, e, gain).astype(_BF16)
            qa_ref[:, lo:NA_W] = pn[:, :NA_W - lo]
            ka_ref[:, 0:hi - NA_W] = pn[:, NA_W - lo:]

    va_ref[...] = _dot(h, w_ref[:, _VA:_VA + NA_W]).astype(_BF16)

    cos = cos_ref[...]
    sin = sin_ref[...]
    first_half = (_lane_id(cos.shape) % HEAD_DIM) < (HEAD_DIM // 2)
    for k in range(2):
        p = _dot(h, w_ref[:, _QB + k * MXU_DIM:_QB + (k + 1) * MXU_DIM])
        gain = gain_ref[2:3, :]
        if k == 1:
            gain = jnp.concatenate([gain_ref[2:3, :LANES], gain_ref[3:4, :LANES]], axis=1)
        pn = _head_norm(p, e, gain)
        for j in range(2):
            xc = pn[:, j * LANES:(j + 1) * LANES]
            swapped = jnp.where(first_half, pltpu.roll(xc, LANES - HEAD_DIM // 2, 1),
                                pltpu.roll(xc, HEAD_DIM // 2, 1))
            xr = (xc * cos + swapped * sin).astype(_BF16)
            col = k * MXU_DIM + j * LANES
            if col < WG_W:
                qb_ref[:, col:col + LANES] = xr
            else:
                kb_ref[...] = xr

    p = _dot(h, w_ref[:, _VB:_VB + WG_KV_W])
    vb_ref[...] = p.astype(_BF16)
    u = _dot(h, w_ref[:, _U:_U + SC_CH])
    bg_ref[...] = _dot(h, w_ref[:, _BG:_BG + SC_CH])
    cu_ref[...] = _dot(h, w_ref[:, _CG:_CG + SC_CH]) * u


def _inproj_call(x, mod, g, w_in, gains, e, cos, sin, tile):
    b, s, _ = x.shape
    nt = s // tile
    tok = lambda width: pl.BlockSpec((None, tile, width), lambda bi, i: (bi, i, 0))
    const = lambda shape: pl.BlockSpec(shape, lambda bi, i: (0,) * len(shape))
    widths = (NA_W, NA_W, NA_W, WG_W, WG_KV_W, WG_KV_W, SC_CH, SC_CH)
    dtypes = (_BF16,) * 6 + (_F32, _F32)
    return pl.pallas_call(
        _inproj_kernel,
        grid=(b, nt),
        in_specs=[
            tok(D_MODEL),
            pl.BlockSpec((None, 6, D_MODEL), lambda bi, i: (bi, 0, 0)),
            const((1, D_MODEL)),
            const((D_MODEL, IN_W)),
            const((SUBLANES, MXU_DIM)),
            const((MXU_DIM, MXU_DIM)),
            pl.BlockSpec((tile, LANES), lambda bi, i: (i, 0)),
            pl.BlockSpec((tile, LANES), lambda bi, i: (i, 0)),
        ],
        out_specs=[tok(w) for w in widths],
        out_shape=[jax.ShapeDtypeStruct((b, s, w), d) for w, d in zip(widths, dtypes)],
        compiler_params=_cparams(2),
        name="mixer_inproj",
    )(x, mod, g, w_in, gains, e, cos, sin)


def _masked_halves(qp):
    qf = qp.astype(_F32)
    low = _lane_id(qf.shape) < HEAD_DIM
    return (jnp.where(low, qf, 0.0).astype(_BF16), jnp.where(low, 0.0, qf).astype(_BF16))


def _softmax_parts(parts, extra=None):
    m = parts[0].max(axis=-1, keepdims=True)
    for s in parts[1:]:
        m = jnp.maximum(m, s.max(axis=-1, keepdims=True))
    if extra is not None:
        m = jnp.maximum(m, extra)
    ps = [jnp.exp(s - m) for s in parts]
    denom = ps[0].sum(axis=-1, keepdims=True)
    for p in ps[1:]:
        denom = denom + p.sum(axis=-1, keepdims=True)
    if extra is not None:
        denom = denom + jnp.exp(extra - m)
    return [p.astype(_BF16) for p in ps], denom


def _na_kernel(q_ref, k_ref, v_ref, kc_ref, vc_ref, bias_ref, o_ref, *, rows_per_step, n_rows):
    step = pl.program_id(1)
    wr = min(NA_WIN_R, n_rows)

    def row_body(rr, carry):
        r = step * rows_per_step + rr
        r0 = jnp.clip(r - wr // 2, 0, n_rows - wr)
        off = r0 - r + NA_WIN_R - 1
        kstart = pl.multiple_of(r0 * GRID_W, GRID_W)
        qstart = pl.multiple_of(rr * GRID_W, GRID_W)
        for c in range(NA_HEADS // 2):
            cols = slice(c * LANES, (c + 1) * LANES)
            q_lo, q_hi = _masked_halves(q_ref[pl.ds(qstart, GRID_W), cols])
            lhs = jnp.concatenate([q_lo, q_hi], axis=0)
            kl = k_ref[pl.ds(kstart, wr * GRID_W), cols]
            vl = v_ref[pl.ds(kstart, wr * GRID_W), cols]
            bias = jnp.concatenate([bias_ref[off, 2 * c], bias_ref[off, 2 * c + 1]], axis=0)
            s_loc = _dot_nt(lhs, kl) + bias
            s_ctx = _dot_nt(lhs, kc_ref[:, cols])
            (p_loc, p_ctx), denom = _softmax_parts([s_loc, s_ctx])
            o = (_dot(p_loc, vl) + _dot(p_ctx, vc_ref[:, cols])) / denom
            low = _lane_id((GRID_W, LANES)) < HEAD_DIM
            o_ref[pl.ds(qstart, GRID_W), cols] = jnp.where(
                low, o[:GRID_W], o[GRID_W:]).astype(_BF16)
        return carry

    lax.fori_loop(0, rows_per_step, row_body, 0)


def _na_call(qa, ka, va, kc, vc, bias):
    b, s, _ = qa.shape
    n_rows = s // GRID_W
    c_len = kc.shape[1]
    rps = NA_ROWS_PER_STEP
    blk = rps * GRID_W
    return pl.pallas_call(
        functools.partial(_na_kernel, rows_per_step=rps, n_rows=n_rows),
        grid=(b, n_rows // rps),
        in_specs=[
            pl.BlockSpec((None, blk, NA_W), lambda bi, j: (bi, j, 0)),
            pl.BlockSpec((None, s, NA_W), lambda bi, j: (bi, 0, 0)),
            pl.BlockSpec((None, s, NA_W), lambda bi, j: (bi, 0, 0)),
            pl.BlockSpec((None, c_len, NA_W), lambda bi, j: (bi, 0, 0)),
            pl.BlockSpec((None, c_len, NA_W), lambda bi, j: (bi, 0, 0)),
            pl.BlockSpec(bias.shape, lambda bi, j: (0, 0, 0, 0)),
        ],
        out_specs=pl.BlockSpec((None, blk, NA_W), lambda bi, j: (bi, j, 0)),
        out_shape=jax.ShapeDtypeStruct((b, s, NA_W), _BF16),
        compiler_params=_cparams(2),
        name="neighbourhood_attn",
    )(qa, ka, va, kc, vc, bias)


def _wg_kernel(sink_ref, q_ref, k_ref, v_ref, kc_ref, vc_ref, o_ref, *, seq):
    i = pl.program_id(1)
    span = 3 * WG_BLOCK
    start = pl.multiple_of(jnp.clip((i - 1) * WG_BLOCK, 0, seq - span), WG_BLOCK)
    kl = k_ref[pl.ds(start, span), :]
    vl = v_ref[pl.ds(start, span), :]
    kc = kc_ref[...]
    vc = vc_ref[...]
    q_tok = i * WG_BLOCK + lax.broadcasted_iota(jnp.int32, (WG_BLOCK, span), 0)
    k_tok = start + lax.broadcasted_iota(jnp.int32, (WG_BLOCK, span), 1)
    mask_add = jnp.where(jnp.abs(q_tok - k_tok) <= WG_WINDOW, 0.0, NEG).astype(_F32)

    pieces = []
    for c in range(WG_HEADS // 2):
        pieces.extend(_masked_halves(q_ref[:, c * LANES:(c + 1) * LANES]))
    lhs = jnp.concatenate(pieces, axis=0)
    s_loc = _dot_nt(lhs, kl)
    s_ctx = _dot_nt(lhs, kc)
    p_loc, p_ctx, denoms = [], [], []
    for hb in range(WG_HEADS):
        rows = slice(hb * WG_BLOCK, (hb + 1) * WG_BLOCK)
        head = (hb // 2) + (WG_HEADS // 2) * (hb % 2)
        (pl_, pc_), dn = _softmax_parts([s_loc[rows] + mask_add, s_ctx[rows]], sink_ref[head])
        p_loc.append(pl_)
        p_ctx.append(pc_)
        denoms.append(dn)
    o = _dot(jnp.concatenate(p_loc, axis=0), vl) + _dot(jnp.concatenate(p_ctx, axis=0), vc)
    low = _lane_id((WG_BLOCK, LANES)) < HEAD_DIM
    for c in range(WG_HEADS // 2):
        o_lo = o[(2 * c) * WG_BLOCK:(2 * c + 1) * WG_BLOCK] / denoms[2 * c]
        o_hi = o[(2 * c + 1) * WG_BLOCK:(2 * c + 2) * WG_BLOCK] / denoms[2 * c + 1]
        o_ref[:, c * LANES:(c + 1) * LANES] = jnp.where(low, o_lo, o_hi).astype(_BF16)


def _wg_call(sink, qb, kb, vb, kc, vc):
    b, s, _ = qb.shape
    c_len = kc.shape[1]
    return pl.pallas_call(
        functools.partial(_wg_kernel, seq=s),
        grid=(b, s // WG_BLOCK),
        in_specs=[
            pl.BlockSpec(memory_space=pltpu.SMEM),
            pl.BlockSpec((None, WG_BLOCK, WG_W), lambda bi, i: (bi, i, 0)),
            pl.BlockSpec((None, s, WG_KV_W), lambda bi, i: (bi, 0, 0)),
            pl.BlockSpec((None, s, WG_KV_W), lambda bi, i: (bi, 0, 0)),
            pl.BlockSpec((None, c_len, WG_KV_W), lambda bi, i: (bi, 0, 0)),
            pl.BlockSpec((None, c_len, WG_KV_W), lambda bi, i: (bi, 0, 0)),
        ],
        out_specs=pl.BlockSpec((None, WG_BLOCK, WG_W), lambda bi, i: (bi, i, 0)),
        out_shape=jax.ShapeDtypeStruct((b, s, WG_W), _BF16),
        compiler_params=_cparams(2),
        name="window_gqa",
    )(sink, qb, kb, vb, kc, vc)


def _ctx_attn_kernel(sink_ref, qa_ref, ka_ref, va_ref, qb_ref, kb_ref, vb_ref, ya_ref, yb_ref):
    c_len = qa_ref.shape[0]
    low = _lane_id((c_len, LANES)) < HEAD_DIM
    for c in range(NA_HEADS // 2):
        cols = slice(c * LANES, (c + 1) * LANES)
        lhs = jnp.concatenate(_masked_halves(qa_ref[:, cols]), axis=0)
        s = _dot_nt(lhs, ka_ref[:, cols])
        (p,), denom = _softmax_parts([s])
        o = _dot(p, va_ref[:, cols]) / denom
        ya_ref[:, cols] = jnp.where(low, o[:c_len], o[c_len:]).astype(_BF16)
    kb = kb_ref[...]
    vb = vb_ref[...]
    for c in range(WG_HEADS // 2):
        cols = slice(c * LANES, (c + 1) * LANES)
        halves = _masked_halves(qb_ref[:, cols])
        outs = []
        for half in range(2):
            head = c + (WG_HEADS // 2) * half
            s = _dot_nt(halves[half], kb)
            (p,), denom = _softmax_parts([s], sink_ref[head])
            outs.append(_dot(p, vb) / denom)
        yb_ref[:, cols] = jnp.where(low, outs[0], outs[1]).astype(_BF16)


def _ctx_attn_call(sink, qa, ka, va, qb, kb, vb):
    b, c_len, _ = qa.shape
    spec = lambda w: pl.BlockSpec((None, c_len, w), lambda bi: (bi, 0, 0))
    return pl.pallas_call(
        _ctx_attn_kernel,
        grid=(b,),
        in_specs=[pl.BlockSpec(memory_space=pltpu.SMEM), spec(NA_W), spec(NA_W), spec(NA_W),
                  spec(WG_W), spec(WG_KV_W), spec(WG_KV_W)],
        out_specs=[spec(NA_W), spec(WG_W)],
        out_shape=[jax.ShapeDtypeStruct((b, c_len, NA_W), _BF16),
                   jax.ShapeDtypeStruct((b, c_len, WG_W), _BF16)],
        compiler_params=_cparams(1),
        name="context_attn",
    )(sink, qa, ka, va, qb, kb, vb)


def _shift_rows(ext, tile):
    n = ext.shape[0]
    prev = pltpu.roll(ext, 1, 0)[SUBLANES:SUBLANES + tile]
    nxt = pltpu.roll(ext, n - 1, 0)[SUBLANES:SUBLANES + tile]
    return prev, nxt


def _outproj_kernel(ya_ref, yb_ref, cu_ref, cu_prev_ref, cu_next_ref, bg_ref, x_ref, mod_ref,
                    conv_ref, wo_ref, o_ref):
    i = pl.program_id(1)
    tile = cu_ref.shape[0]
    cu = cu_ref[...]
    prev_ok = (i > 0).astype(_F32)
    next_ok = (i < pl.num_programs(1) - 1).astype(_F32)
    ext = jnp.concatenate([cu_prev_ref[...] * prev_ok, cu, cu_next_ref[...] * next_ok], axis=0)
    prev, nxt = _shift_rows(ext, tile)
    yc = bg_ref[...] * (prev * conv_ref[0:1, :] + cu * conv_ref[1:2, :] + nxt * conv_ref[2:3, :])
    y = (_dot(ya_ref[...], wo_ref[0:NA_W, :])
         + _dot(yb_ref[...], wo_ref[NA_W:NA_W + WG_W, :])
         + _dot(yc.astype(_BF16), wo_ref[NA_W + WG_W:, :]))
    o_ref[...] = x_ref[...] + mod_ref[2:3, :] * y


def _halo_specs(tile, width, n_rows):
    per = tile // SUBLANES
    last = n_rows // SUBLANES - 1
    prev = pl.BlockSpec((None, SUBLANES, width),
                        lambda bi, i: (bi, jnp.maximum(i * per - 1, 0), 0))
    nxt = pl.BlockSpec((None, SUBLANES, width),
                       lambda bi, i: (bi, jnp.minimum((i + 1) * per, last), 0))
    return prev, nxt


def _outproj_call(ya, yb, cu, bg, x, mod, conv_c, w_o, tile):
    b, s, _ = x.shape
    tok = lambda width: pl.BlockSpec((None, tile, width), lambda bi, i: (bi, i, 0))
    const = lambda shape: pl.BlockSpec(shape, lambda bi, i: (0,) * len(shape))
    prev, nxt = _halo_specs(tile, SC_CH, s)
    return pl.pallas_call(
        _outproj_kernel,
        grid=(b, s // tile),
        in_specs=[tok(NA_W), tok(WG_W), tok(SC_CH), prev, nxt, tok(SC_CH), tok(D_MODEL),
                  pl.BlockSpec((None, 6, D_MODEL), lambda bi, i: (bi, 0, 0)),
                  const((SUBLANES, SC_CH)), const((D_MODEL, D_MODEL))],
        out_specs=tok(D_MODEL),
        out_shape=jax.ShapeDtypeStruct((b, s, D_MODEL), _F32),
        compiler_params=_cparams(2),
        name="mixer_outproj",
    )(ya, yb, cu, cu, cu, bg, x, mod, conv_c, w_o)


def _ffn_kernel(x_ref, xp_ref, xn_ref, mod_ref, g_ref, wa_ref, wg_ref, conv_ref, wd_ref, o_ref,
                h_scr, acc_scr):
    i = pl.program_id(1)
    tile = x_ref.shape[0]
    g = g_ref[...]
    shift, scale = mod_ref[3:4, :], mod_ref[4:5, :]
    prev_ok = (i > 0).astype(_F32)
    next_ok = (i < pl.num_programs(1) - 1).astype(_F32)
    h_scr[...] = jnp.concatenate(
        [_norm_mod(xp_ref[...], g, shift, scale) * prev_ok,
         _norm_mod(x_ref[...], g, shift, scale),
         _norm_mod(xn_ref[...], g, shift, scale) * next_ok], axis=0).astype(_BF16)
    acc_scr[...] = jnp.zeros_like(acc_scr)

    def chunk(c, carry):
        h = h_scr[...]
        taps = conv_ref[c]
        ua = _dot(h, wa_ref[c])
        ug = _dot(h, wg_ref[c])
        pa, na = _shift_rows(ua, tile)
        pg, ng = _shift_rows(ug, tile)
        mid = slice(SUBLANES, SUBLANES + tile)
        a = pa * taps[0:1, :] + ua[mid] * taps[1:2, :] + na * taps[2:3, :]
        gt = pg * taps[3:4, :] + ug[mid] * taps[4:5, :] + ng * taps[5:6, :]
        act = (a / (1.0 + jnp.exp(-a))) * gt
        acc_scr[...] += _dot(act.astype(_BF16), wd_ref[c])
        return carry

    lax.fori_loop(0, N_FF_CHUNKS, chunk, 0)
    o_ref[...] = x_ref[...] + mod_ref[5:6, :] * acc_scr[...]


def _ffn_call(x, mod, g, wa, wg, conv, wd, tile):
    b, s, _ = x.shape
    tok = pl.BlockSpec((None, tile, D_MODEL), lambda bi, i: (bi, i, 0))
    const = lambda shape: pl.BlockSpec(shape, lambda bi, i: (0,) * len(shape),
                                       pipeline_mode=pl.Buffered(1))
    prev, nxt = _halo_specs(tile, D_MODEL, s)
    return pl.pallas_call(
        _ffn_kernel,
        grid=(b, s // tile),
        in_specs=[tok, prev, nxt,
                  pl.BlockSpec((None, 6, D_MODEL), lambda bi, i: (bi, 0, 0)),
                  pl.BlockSpec((1, D_MODEL), lambda bi, i: (0, 0)),
                  const(wa.shape), const(wg.shape), const(conv.shape), const(wd.shape)],
        out_specs=tok,
        out_shape=jax.ShapeDtypeStruct((b, s, D_MODEL), _F32),
        scratch_shapes=[pltpu.VMEM((tile + 2 * SUBLANES, D_MODEL), _BF16),
                        pltpu.VMEM((tile, D_MODEL), _F32)],
        compiler_params=_cparams(2),
        name="conv_ffn",
    )(x, x, x, mod, g, wa, wg, conv, wd)


def _wg_head_perm():
    order = []
    for c in range(WG_HEADS // 2):
        order += [c, c + WG_HEADS // 2]
    return np.concatenate([np.arange(h * HEAD_DIM, (h + 1) * HEAD_DIM) for h in order])


def _rope_tables(seq):
    t = np.arange(seq)
    row = (t // GRID_W).astype(np.float64)
    col = (t % GRID_W).astype(np.float64)
    half = HEAD_DIM // 2
    n_freq = half // 2
    inv = ROPE_BASE ** (-np.arange(n_freq, dtype=np.float64) / n_freq)
    ang = np.concatenate([row[:, None] * inv, col[:, None] * inv], axis=-1)
    cos = np.concatenate([np.cos(ang), np.cos(ang)], axis=-1)
    sin = np.concatenate([-np.sin(ang), np.sin(ang)], axis=-1)
    reps = LANES // HEAD_DIM
    return (jnp.asarray(np.tile(cos, (1, reps)), _F32), jnp.asarray(np.tile(sin, (1, reps)), _F32))


def _na_bias_table(rpb, n_rows):
    wr = min(NA_WIN_R, n_rows)
    q = np.arange(GRID_W)
    c0 = np.clip(q - NA_WIN_C // 2, 0, GRID_W - NA_WIN_C)
    kc = np.arange(GRID_W)
    ok = (kc[None, :] >= c0[:, None]) & (kc[None, :] < c0[:, None] + NA_WIN_C)
    dcol = np.clip(kc[None, :] - q[:, None] + NA_WIN_C - 1, 0, 2 * NA_WIN_C - 2)
    offs = np.arange(NA_WIN_R)
    drow = np.clip(offs[:, None] + np.arange(wr)[None, :], 0, 2 * NA_WIN_R - 2)
    tab = rpb[:, drow[:, :, None, None], dcol[None, None, :, :]]
    tab = jnp.where(ok[None, None, None], tab.astype(_F32), NEG)
    tab = jnp.transpose(tab, (1, 0, 3, 2, 4))
    return tab.reshape(NA_WIN_R, NA_HEADS, GRID_W, wr * GRID_W)


def _block_diag_mean():
    idx = np.arange(MXU_DIM) // HEAD_DIM
    return jnp.asarray((idx[:, None] == idx[None, :]) / HEAD_DIM, _BF16)


def _pad_rows(a, rows):
    return jnp.concatenate([a, jnp.zeros((rows - a.shape[0],) + a.shape[1:], a.dtype)], axis=0)


def kernel(x, c, ctx, c_ctx, w_ada, b_ada, g_attn, w_in, qn_a, kn_a, qn_b, kn_b, rpb_a, sink_b,
           conv_c, w_o, g_ffn, w_up, conv_ffn, w_down):
    batch, seq, _ = x.shape
    c_len = ctx.shape[1]
    n_rows = seq // GRID_W
    scale = 1.0 / math.sqrt(HEAD_DIM)
    perm = _wg_head_perm()

    cvec = _pad_rows(jnp.concatenate([c, c_ctx[None, :]], axis=0), 2 * SUBLANES)
    mod = _ada_call(cvec, w_ada, b_ada)

    cos, sin = _rope_tables(seq)
    ones_c = jnp.ones((c_len, LANES), _F32)
    zeros_c = jnp.zeros((c_len, LANES), _F32)
    e = _block_diag_mean()

    xl, xc = x, ctx
    for l in range(DEPTH):
        update_ctx = l < DEPTH - 1
        mod_l = mod[l, :batch].reshape(batch, 6, D_MODEL)
        mod_c = jnp.broadcast_to(mod[l, batch].reshape(1, 6, D_MODEL), (batch, 6, D_MODEL))

        w_in_l = w_in[l]
        w_in_l = jnp.concatenate(
            [w_in_l[:, :_QB], w_in_l[:, _QB:_KB][:, perm], w_in_l[:, _KB:]], axis=1).astype(_BF16)
        w_o_l = w_o[l]
        w_o_l = jnp.concatenate(
            [w_o_l[:NA_W], w_o_l[NA_W:NA_W + WG_W][perm], w_o_l[NA_W + WG_W:]], axis=0).astype(_BF16)
        tile4 = lambda v: jnp.tile(v, MXU_DIM // HEAD_DIM)
        gains = _pad_rows(jnp.stack([tile4(qn_a[l]) * scale, tile4(kn_a[l]),
                                     tile4(qn_b[l]) * scale, tile4(kn_b[l])]), SUBLANES)
        bias = _na_bias_table(rpb_a[l], n_rows)
        conv_c_l = _pad_rows(conv_c[l], SUBLANES)
        g_attn_l = g_attn[l].reshape(1, D_MODEL)
        g_ffn_l = g_ffn[l].reshape(1, D_MODEL)
        wa = jnp.transpose(w_up[l][:, :D_FF].reshape(D_MODEL, N_FF_CHUNKS, FF_CHUNK),
                           (1, 0, 2)).astype(_BF16)
        wg = jnp.transpose(w_up[l][:, D_FF:].reshape(D_MODEL, N_FF_CHUNKS, FF_CHUNK),
                           (1, 0, 2)).astype(_BF16)
        wd = w_down[l].reshape(N_FF_CHUNKS, FF_CHUNK, D_MODEL).astype(_BF16)
        taps = jnp.concatenate([conv_ffn[l][:, :D_FF], conv_ffn[l][:, D_FF:]], axis=0)
        taps = jnp.transpose(_pad_rows(taps, SUBLANES).reshape(SUBLANES, N_FF_CHUNKS, FF_CHUNK),
                             (1, 0, 2))
        sink_perm = sink_b[l]

        qa, ka, va, qb, kb, vb, cu, bg = _inproj_call(
            xl, mod_l, g_attn_l, w_in_l, gains, e, cos, sin, TOKEN_TILE)
        qa_c, ka_c, va_c, qb_c, kb_c, vb_c, cu_c, bg_c = _inproj_call(
            xc, mod_c, g_attn_l, w_in_l, gains, e, ones_c, zeros_c, c_len)
        ya = _na_call(qa, ka, va, ka_c, va_c, bias)
        yb = _wg_call(sink_perm, qb, kb, vb, kb_c, vb_c)
        xl = _outproj_call(ya, yb, cu, bg, xl, mod_l, conv_c_l, w_o_l, TOKEN_TILE)
        if update_ctx:
            ya_c, yb_c = _ctx_attn_call(sink_perm, qa_c, ka_c, va_c, qb_c, kb_c, vb_c)
            xc = _outproj_call(ya_c, yb_c, cu_c, bg_c, xc, mod_c, conv_c_l, w_o_l, c_len)

        xl = _ffn_call(xl, mod_l, g_ffn_l, wa, wg, taps, wd, TOKEN_TILE)
        if update_ctx:
            xc = _ffn_call(xc, mod_c, g_ffn_l, wa, wg, taps, wd, c_len)
    return xl
```

```python
import functools
import math

import jax
import jax.numpy as jnp
import numpy as np
from jax import lax
from jax.experimental import pallas as pl
from jax.experimental.pallas import tpu as pltpu

D_MODEL = 1024
DEPTH = 2
GRID_W = 64
HEAD_DIM = 64
NA_HEADS = 6
NA_WIN_R = 8
NA_WIN_C = 16
WG_HEADS = 6
WG_KV_HEADS = 2
WG_WINDOW = 128
WG_BLOCK = 128
SC_CH = 256
CONV_W = 3
D_FF = 2816
ROPE_BASE = 10000.0
EPS = 1e-6
NEG = -1e30

NA_W = NA_HEADS * HEAD_DIM
WG_W = WG_HEADS * HEAD_DIM
WG_KV_W = WG_KV_HEADS * HEAD_DIM
IN_W = 3 * NA_W + WG_W + 2 * WG_KV_W + 3 * SC_CH

LANES = 128
SUBLANES = 8
MXU_DIM = 256
FF_CHUNK = MXU_DIM
N_FF_CHUNKS = D_FF // FF_CHUNK
TOKEN_TILE = 512
INPROJ_SUB_TILE = 256
NA_ROWS_PER_STEP = 4
WG_BLOCKS_PER_STEP = 2
VMEM_LIMIT = 48 * 1024 * 1024

_QA, _KA, _VA = 0, NA_W, 2 * NA_W
_QB = 3 * NA_W
_KB = _QB + WG_W
_VB = _KB + WG_KV_W
_U = _VB + WG_KV_W
_BG = _U + SC_CH
_CG = _BG + SC_CH

_F32 = jnp.float32
_BF16 = jnp.bfloat16


def _dot(a, b):
    return jnp.dot(a, b, preferred_element_type=_F32)


def _dot_nt(a, b):
    return lax.dot_general(a, b, (((1,), (1,)), ((), ())), preferred_element_type=_F32)


def _cparams(n_grid):
    return pltpu.CompilerParams(
        dimension_semantics=("arbitrary",) * n_grid, vmem_limit_bytes=VMEM_LIMIT)


def _ada_kernel(c_ref, w_ref, b_ref, o_ref):
    c = c_ref[...]
    s = c / (1.0 + jnp.exp(-c))
    s_hi = s.astype(_BF16)
    s_lo = (s - s_hi.astype(_F32)).astype(_BF16)
    w = w_ref[...]
    w_hi = w.astype(_BF16)
    w_lo = (w - w_hi.astype(_F32)).astype(_BF16)
    o_ref[...] = _dot(s_hi, w_hi) + _dot(s_lo, w_hi) + _dot(s_hi, w_lo) + b_ref[...]


def _ada_call(cvec, w_ada, b_ada):
    rows = cvec.shape[0]
    n_out = w_ada.shape[-1]
    nt = 1536
    return pl.pallas_call(
        _ada_kernel,
        grid=(DEPTH, n_out // nt),
        in_specs=[
            pl.BlockSpec((rows, D_MODEL), lambda l, j: (0, 0)),
            pl.BlockSpec((None, D_MODEL, nt), lambda l, j: (l, 0, j)),
            pl.BlockSpec((None, 1, nt), lambda l, j: (l, 0, j)),
        ],
        out_specs=pl.BlockSpec((None, rows, nt), lambda l, j: (l, 0, j)),
        out_shape=jax.ShapeDtypeStruct((DEPTH, rows, n_out), _F32),
        compiler_params=_cparams(2),
        name="adaln_mod",
    )(cvec, w_ada, b_ada.reshape(DEPTH, 1, n_out))


def _norm_mod(x, g, shift, scale):
    ms = jnp.mean(x * x, axis=-1, keepdims=True)
    return (x * lax.rsqrt(ms + EPS) * g) * (1.0 + scale) + shift


def _head_norm(x, e, gain):
    sq = x * x
    hi = sq.astype(_BF16)
    lo = (sq - hi.astype(_F32)).astype(_BF16)
    ms = _dot(hi, e) + _dot(lo, e)
    return x * lax.rsqrt(ms + EPS) * gain


def _lane_id(shape):
    return lax.broadcasted_iota(jnp.int32, shape, len(shape) - 1)


def _inproj_kernel(x_ref, mod_ref, g_ref, w_ref, gain_ref, e_ref, cos_ref, sin_ref,
                   qa_ref, ka_ref, va_ref, qb_ref, kb_ref, vb_ref, cu_ref, bg_ref, p_scr, *, sub):
    e = e_ref[...]
    gain_qk_a = jnp.concatenate([gain_ref[0:1, :LANES], gain_ref[1:2, :LANES]], axis=1)
    gain_qk_b = jnp.concatenate([gain_ref[2:3, :LANES], gain_ref[3:4, :LANES]], axis=1)
    first_half = (_lane_id((sub, LANES)) % HEAD_DIM) < (HEAD_DIM // 2)
    for t in range(x_ref.shape[0] // sub):
        rows = slice(t * sub, (t + 1) * sub)
        h = _norm_mod(x_ref[rows, :], g_ref[...], mod_ref[0:1, :], mod_ref[1:2, :]).astype(_BF16)
        p_scr[rows, :] = _dot(h, w_ref[...])

        p = p_scr[rows, _QA:_QA + MXU_DIM]
        qa_ref[rows, 0:MXU_DIM] = _head_norm(p, e, gain_ref[0:1, :]).astype(_BF16)
        p = p_scr[rows, _QA + MXU_DIM:_QA + 2 * MXU_DIM]
        pn = _head_norm(p, e, gain_qk_a).astype(_BF16)
        qa_ref[rows, MXU_DIM:NA_W] = pn[:, :LANES]
        ka_ref[rows, 0:LANES] = pn[:, LANES:]
        p = p_scr[rows, _QA + 2 * MXU_DIM:_QA + 3 * MXU_DIM]
        ka_ref[rows, LANES:NA_W] = _head_norm(p, e, gain_ref[1:2, :]).astype(_BF16)

        va_ref[rows, :] = p_scr[rows, _VA:_VA + NA_W].astype(_BF16)

        cos = cos_ref[rows, :]
        sin = sin_ref[rows, :]
        for k in range(2):
            p = p_scr[rows, _QB + k * MXU_DIM:_QB + (k + 1) * MXU_DIM]
            pn = _head_norm(p, e, gain_ref[2:3, :] if k == 0 else gain_qk_b)
            for j in range(2):
                xc = pn[:, j * LANES:(j + 1) * LANES]
                swapped = jnp.where(first_half, pltpu.roll(xc, LANES - HEAD_DIM // 2, 1),
                                    pltpu.roll(xc, HEAD_DIM // 2, 1))
                xr = (xc * cos + swapped * sin).astype(_BF16)
                col = k * MXU_DIM + j * LANES
                if col < WG_W:
                    qb_ref[rows, col:col + LANES] = xr
                else:
                    kb_ref[rows, :] = xr

        vb_ref[rows, :] = p_scr[rows, _VB:_VB + WG_KV_W].astype(_BF16)
        bg_ref[rows, :] = p_scr[rows, _BG:_BG + SC_CH]
        cu_ref[rows, :] = p_scr[rows, _CG:_CG + SC_CH] * p_scr[rows, _U:_U + SC_CH]


def _inproj_call(x, mod, g, w_in, gains, e, cos, sin, tile):
    b, s, _ = x.shape
    nt = s // tile
    tok = lambda width: pl.BlockSpec((None, tile, width), lambda bi, i: (bi, i, 0))
    const = lambda shape: pl.BlockSpec(shape, lambda bi, i: (0,) * len(shape))
    widths = (NA_W, NA_W, NA_W, WG_W, WG_KV_W, WG_KV_W, SC_CH, SC_CH)
    dtypes = (_BF16,) * 6 + (_F32, _F32)
    return pl.pallas_call(
        functools.partial(_inproj_kernel, sub=min(tile, INPROJ_SUB_TILE)),
        grid=(b, nt),
        in_specs=[
            tok(D_MODEL),
            pl.BlockSpec((None, 6, D_MODEL), lambda bi, i: (bi, 0, 0)),
            const((1, D_MODEL)),
            const((D_MODEL, IN_W)),
            const((SUBLANES, MXU_DIM)),
            const((MXU_DIM, MXU_DIM)),
            pl.BlockSpec((tile, LANES), lambda bi, i: (i, 0)),
            pl.BlockSpec((tile, LANES), lambda bi, i: (i, 0)),
        ],
        out_specs=[tok(w) for w in widths],
        out_shape=[jax.ShapeDtypeStruct((b, s, w), d) for w, d in zip(widths, dtypes)],
        scratch_shapes=[pltpu.VMEM((tile, IN_W), _F32)],
        compiler_params=_cparams(2),
        name="mixer_inproj",
    )(x, mod, g, w_in, gains, e, cos, sin)


def _masked_halves(qp):
    qf = qp.astype(_F32)
    low = _lane_id(qf.shape) < HEAD_DIM
    return (jnp.where(low, qf, 0.0).astype(_BF16), jnp.where(low, 0.0, qf).astype(_BF16))


def _softmax_parts(parts, extra=None):
    m = parts[0].max(axis=-1, keepdims=True)
    for s in parts[1:]:
        m = jnp.maximum(m, s.max(axis=-1, keepdims=True))
    if extra is not None:
        m = jnp.maximum(m, extra)
    ps = [jnp.exp(s - m) for s in parts]
    denom = ps[0].sum(axis=-1, keepdims=True)
    for p in ps[1:]:
        denom = denom + p.sum(axis=-1, keepdims=True)
    if extra is not None:
        denom = denom + jnp.exp(extra - m)
    return [p.astype(_BF16) for p in ps], denom


def _na_kernel(q_ref, k_ref, v_ref, kc_ref, vc_ref, bias_ref, o_ref, *, rows_per_step, n_rows):
    step = pl.program_id(1)
    wr = min(NA_WIN_R, n_rows)
    low = _lane_id((GRID_W, LANES)) < HEAD_DIM
    offs, kstarts = [], []
    for rr in range(rows_per_step):
        r = step * rows_per_step + rr
        r0 = jnp.clip(r - wr // 2, 0, n_rows - wr)
        offs.append(r0 - r + NA_WIN_R - 1)
        kstarts.append(pl.multiple_of(r0 * GRID_W, GRID_W))

    for c in range(NA_HEADS // 2):
        cols = slice(c * LANES, (c + 1) * LANES)
        lhs = jnp.concatenate(
            [half for rr in range(rows_per_step)
             for half in _masked_halves(q_ref[rr * GRID_W:(rr + 1) * GRID_W, cols])], axis=0)
        s_ctx = _dot_nt(lhs, kc_ref[:, cols])
        p_ctx, o_loc, denoms = [], [], []
        for rr in range(rows_per_step):
            rows = slice(rr * 2 * GRID_W, (rr + 1) * 2 * GRID_W)
            kl = k_ref[pl.ds(kstarts[rr], wr * GRID_W), cols]
            vl = v_ref[pl.ds(kstarts[rr], wr * GRID_W), cols]
            bias = jnp.concatenate(
                [bias_ref[offs[rr], 2 * c], bias_ref[offs[rr], 2 * c + 1]], axis=0)
            s_loc = _dot_nt(lhs[rows], kl) + bias
            (pl_, pc_), dn = _softmax_parts([s_loc, s_ctx[rows]])
            o_loc.append(_dot(pl_, vl))
            p_ctx.append(pc_)
            denoms.append(dn)
        o_ctx = _dot(jnp.concatenate(p_ctx, axis=0), vc_ref[:, cols])
        for rr in range(rows_per_step):
            rows = slice(rr * 2 * GRID_W, (rr + 1) * 2 * GRID_W)
            o = (o_loc[rr] + o_ctx[rows]) / denoms[rr]
            o_ref[rr * GRID_W:(rr + 1) * GRID_W, cols] = jnp.where(
                low, o[:GRID_W], o[GRID_W:]).astype(_BF16)


def _na_call(qa, ka, va, kc, vc, bias):
    b, s, _ = qa.shape
    n_rows = s // GRID_W
    c_len = kc.shape[1]
    rps = NA_ROWS_PER_STEP
    blk = rps * GRID_W
    return pl.pallas_call(
        functools.partial(_na_kernel, rows_per_step=rps, n_rows=n_rows),
        grid=(b, n_rows // rps),
        in_specs=[
            pl.BlockSpec((None, blk, NA_W), lambda bi, j: (bi, j, 0)),
            pl.BlockSpec((None, s, NA_W), lambda bi, j: (bi, 0, 0)),
            pl.BlockSpec((None, s, NA_W), lambda bi, j: (bi, 0, 0)),
            pl.BlockSpec((None, c_len, NA_W), lambda bi, j: (bi, 0, 0)),
            pl.BlockSpec((None, c_len, NA_W), lambda bi, j: (bi, 0, 0)),
            pl.BlockSpec(bias.shape, lambda bi, j: (0, 0, 0, 0)),
        ],
        out_specs=pl.BlockSpec((None, blk, NA_W), lambda bi, j: (bi, j, 0)),
        out_shape=jax.ShapeDtypeStruct((b, s, NA_W), _BF16),
        compiler_params=_cparams(2),
        name="neighbourhood_attn",
    )(qa, ka, va, kc, vc, bias)


def _wg_kernel(sink_ref, q_ref, k_ref, v_ref, kc_ref, vc_ref, o_ref, *, seq, n_blk):
    step = pl.program_id(1)
    span = 3 * WG_BLOCK
    per_blk = WG_HEADS * WG_BLOCK
    kc = kc_ref[...]
    vc = vc_ref[...]
    low = _lane_id((WG_BLOCK, LANES)) < HEAD_DIM
    row_id = lax.broadcasted_iota(jnp.int32, (WG_BLOCK, span), 0)
    col_id = lax.broadcasted_iota(jnp.int32, (WG_BLOCK, span), 1)

    pieces = []
    for bb in range(n_blk):
        for c in range(WG_HEADS // 2):
            pieces.extend(_masked_halves(
                q_ref[bb * WG_BLOCK:(bb + 1) * WG_BLOCK, c * LANES:(c + 1) * LANES]))
    lhs = jnp.concatenate(pieces, axis=0)
    s_ctx = _dot_nt(lhs, kc)
    p_ctx, o_loc, denoms = [], [], []
    for bb in range(n_blk):
        i = step * n_blk + bb
        start = pl.multiple_of(jnp.clip((i - 1) * WG_BLOCK, 0, seq - span), WG_BLOCK)
        kl = k_ref[pl.ds(start, span), :]
        vl = v_ref[pl.ds(start, span), :]
        dist = (i * WG_BLOCK - start) + row_id - col_id
        mask_add = jnp.where(jnp.abs(dist) <= WG_WINDOW, 0.0, NEG).astype(_F32)
        s_loc = _dot_nt(lhs[bb * per_blk:(bb + 1) * per_blk], kl)
        p_loc = []
        for hb in range(WG_HEADS):
            head = (hb // 2) + (WG_HEADS // 2) * (hb % 2)
            rows = slice(hb * WG_BLOCK, (hb + 1) * WG_BLOCK)
            rows_all = slice(bb * per_blk + hb * WG_BLOCK, bb * per_blk + (hb + 1) * WG_BLOCK)
            (pl_, pc_), dn = _softmax_parts([s_loc[rows] + mask_add, s_ctx[rows_all]],
                                            sink_ref[head])
            p_loc.append(pl_)
            p_ctx.append(pc_)
            denoms.append(dn)
        o_loc.append(_dot(jnp.concatenate(p_loc, axis=0), vl))
    o_ctx = _dot(jnp.concatenate(p_ctx, axis=0), vc)
    for bb in range(n_blk):
        for c in range(WG_HEADS // 2):
            halves = []
            for half in range(2):
                hb = 2 * c + half
                rows = slice(hb * WG_BLOCK, (hb + 1) * WG_BLOCK)
                rows_all = slice(bb * per_blk + hb * WG_BLOCK, bb * per_blk + (hb + 1) * WG_BLOCK)
                halves.append((o_loc[bb][rows] + o_ctx[rows_all]) / denoms[bb * WG_HEADS + hb])
            o_ref[bb * WG_BLOCK:(bb + 1) * WG_BLOCK, c * LANES:(c + 1) * LANES] = jnp.where(
                low, halves[0], halves[1]).astype(_BF16)


def _wg_call(sink, qb, kb, vb, kc, vc):
    b, s, _ = qb.shape
    c_len = kc.shape[1]
    n_blk = WG_BLOCKS_PER_STEP
    q_rows = n_blk * WG_BLOCK
    return pl.pallas_call(
        functools.partial(_wg_kernel, seq=s, n_blk=n_blk),
        grid=(b, s // q_rows),
        in_specs=[
            pl.BlockSpec(memory_space=pltpu.SMEM),
            pl.BlockSpec((None, q_rows, WG_W), lambda bi, i: (bi, i, 0)),
            pl.BlockSpec((None, s, WG_KV_W), lambda bi, i: (bi, 0, 0)),
            pl.BlockSpec((None, s, WG_KV_W), lambda bi, i: (bi, 0, 0)),
            pl.BlockSpec((None, c_len, WG_KV_W), lambda bi, i: (bi, 0, 0)),
            pl.BlockSpec((None, c_len, WG_KV_W), lambda bi, i: (bi, 0, 0)),
        ],
        out_specs=pl.BlockSpec((None, q_rows, WG_W), lambda bi, i: (bi, i, 0)),
        out_shape=jax.ShapeDtypeStruct((b, s, WG_W), _BF16),
        compiler_params=_cparams(2),
        name="window_gqa",
    )(sink, qb, kb, vb, kc, vc)


def _ctx_attn_kernel(sink_ref, qa_ref, ka_ref, va_ref, qb_ref, kb_ref, vb_ref, ya_ref, yb_ref):
    c_len = qa_ref.shape[0]
    low = _lane_id((c_len, LANES)) < HEAD_DIM
    for c in range(NA_HEADS // 2):
        cols = slice(c * LANES, (c + 1) * LANES)
        lhs = jnp.concatenate(_masked_halves(qa_ref[:, cols]), axis=0)
        s = _dot_nt(lhs, ka_ref[:, cols])
        (p,), denom = _softmax_parts([s])
        o = _dot(p, va_ref[:, cols]) / denom
        ya_ref[:, cols] = jnp.where(low, o[:c_len], o[c_len:]).astype(_BF16)
    kb = kb_ref[...]
    vb = vb_ref[...]
    for c in range(WG_HEADS // 2):
        cols = slice(c * LANES, (c + 1) * LANES)
        halves = _masked_halves(qb_ref[:, cols])
        outs = []
        for half in range(2):
            head = c + (WG_HEADS // 2) * half
            s = _dot_nt(halves[half], kb)
            (p,), denom = _softmax_parts([s], sink_ref[head])
            outs.append(_dot(p, vb) / denom)
        yb_ref[:, cols] = jnp.where(low, outs[0], outs[1]).astype(_BF16)


def _ctx_attn_call(sink, qa, ka, va, qb, kb, vb):
    b, c_len, _ = qa.shape
    spec = lambda w: pl.BlockSpec((None, c_len, w), lambda bi: (bi, 0, 0))
    return pl.pallas_call(
        _ctx_attn_kernel,
        grid=(b,),
        in_specs=[pl.BlockSpec(memory_space=pltpu.SMEM), spec(NA_W), spec(NA_W), spec(NA_W),
                  spec(WG_W), spec(WG_KV_W), spec(WG_KV_W)],
        out_specs=[spec(NA_W), spec(WG_W)],
        out_shape=[jax.ShapeDtypeStruct((b, c_len, NA_W), _BF16),
                   jax.ShapeDtypeStruct((b, c_len, WG_W), _BF16)],
        compiler_params=_cparams(1),
        name="context_attn",
    )(sink, qa, ka, va, qb, kb, vb)


def _shift_rows(ext, tile):
    n = ext.shape[0]
    prev = pltpu.roll(ext, 1, 0)[SUBLANES:SUBLANES + tile]
    nxt = pltpu.roll(ext, n - 1, 0)[SUBLANES:SUBLANES + tile]
    return prev, nxt


def _outproj_kernel(ya_ref, yb_ref, cu_ref, cu_prev_ref, cu_next_ref, bg_ref, x_ref, mod_ref,
                    conv_ref, wo_ref, o_ref):
    i = pl.program_id(1)
    tile = cu_ref.shape[0]
    cu = cu_ref[...]
    prev_ok = (i > 0).astype(_F32)
    next_ok = (i < pl.num_programs(1) - 1).astype(_F32)
    ext = jnp.concatenate([cu_prev_ref[...] * prev_ok, cu, cu_next_ref[...] * next_ok], axis=0)
    prev, nxt = _shift_rows(ext, tile)
    yc = bg_ref[...] * (prev * conv_ref[0:1, :] + cu * conv_ref[1:2, :] + nxt * conv_ref[2:3, :])
    y = (_dot(ya_ref[...], wo_ref[0:NA_W, :])
         + _dot(yb_ref[...], wo_ref[NA_W:NA_W + WG_W, :])
         + _dot(yc.astype(_BF16), wo_ref[NA_W + WG_W:, :]))
    o_ref[...] = x_ref[...] + mod_ref[2:3, :] * y


def _halo_specs(tile, width, n_rows):
    per = tile // SUBLANES
    last = n_rows // SUBLANES - 1
    prev = pl.BlockSpec((None, SUBLANES, width),
                        lambda bi, i: (bi, jnp.maximum(i * per - 1, 0), 0))
    nxt = pl.BlockSpec((None, SUBLANES, width),
                       lambda bi, i: (bi, jnp.minimum((i + 1) * per, last), 0))
    return prev, nxt


def _outproj_call(ya, yb, cu, bg, x, mod, conv_c, w_o, tile):
    b, s, _ = x.shape
    tok = lambda width: pl.BlockSpec((None, tile, width), lambda bi, i: (bi, i, 0))
    const = lambda shape: pl.BlockSpec(shape, lambda bi, i: (0,) * len(shape))
    prev, nxt = _halo_specs(tile, SC_CH, s)
    return pl.pallas_call(
        _outproj_kernel,
        grid=(b, s // tile),
        in_specs=[tok(NA_W), tok(WG_W), tok(SC_CH), prev, nxt, tok(SC_CH), tok(D_MODEL),
                  pl.BlockSpec((None, 6, D_MODEL), lambda bi, i: (bi, 0, 0)),
                  const((SUBLANES, SC_CH)), const((D_MODEL, D_MODEL))],
        out_specs=tok(D_MODEL),
        out_shape=jax.ShapeDtypeStruct((b, s, D_MODEL), _F32),
        compiler_params=_cparams(2),
        name="mixer_outproj",
    )(ya, yb, cu, cu, cu, bg, x, mod, conv_c, w_o)


def _ffn_kernel(x_ref, xp_ref, xn_ref, mod_ref, g_ref, wu_ref, conv_ref, wd_ref, o_ref,
                h_scr, act_scr):
    i = pl.program_id(1)
    tile = x_ref.shape[0]
    g = g_ref[...]
    shift, scale = mod_ref[3:4, :], mod_ref[4:5, :]
    prev_ok = (i > 0).astype(_F32)
    next_ok = (i < pl.num_programs(1) - 1).astype(_F32)
    h_scr[...] = jnp.concatenate(
        [_norm_mod(xp_ref[...], g, shift, scale) * prev_ok,
         _norm_mod(x_ref[...], g, shift, scale),
         _norm_mod(xn_ref[...], g, shift, scale) * next_ok], axis=0).astype(_BF16)

    mid = slice(SUBLANES, SUBLANES + tile)
    for c in range(N_FF_CHUNKS):
        a_cols = slice(c * FF_CHUNK, (c + 1) * FF_CHUNK)
        g_cols = slice(D_FF + c * FF_CHUNK, D_FF + (c + 1) * FF_CHUNK)
        h = h_scr[...]
        ua = _dot(h, wu_ref[:, a_cols])
        ug = _dot(h, wu_ref[:, g_cols])
        pa, na = _shift_rows(ua, tile)
        pg, ng = _shift_rows(ug, tile)
        a = (pa * conv_ref[0:1, a_cols] + ua[mid] * conv_ref[1:2, a_cols]
             + na * conv_ref[2:3, a_cols])
        gt = (pg * conv_ref[0:1, g_cols] + ug[mid] * conv_ref[1:2, g_cols]
              + ng * conv_ref[2:3, g_cols])
        act_scr[:, a_cols] = ((a / (1.0 + jnp.exp(-a))) * gt).astype(_BF16)

    o_ref[...] = x_ref[...] + mod_ref[5:6, :] * _dot(act_scr[...], wd_ref[...])


def _ffn_call(x, mod, g, wu, conv, wd, tile):
    b, s, _ = x.shape
    tok = pl.BlockSpec((None, tile, D_MODEL), lambda bi, i: (bi, i, 0))
    const = lambda shape: pl.BlockSpec(shape, lambda bi, i: (0,) * len(shape),
                                       pipeline_mode=pl.Buffered(1))
    prev, nxt = _halo_specs(tile, D_MODEL, s)
    return pl.pallas_call(
        _ffn_kernel,
        grid=(b, s // tile),
        in_specs=[tok, prev, nxt,
                  pl.BlockSpec((None, 6, D_MODEL), lambda bi, i: (bi, 0, 0)),
                  pl.BlockSpec((1, D_MODEL), lambda bi, i: (0, 0)),
                  const(wu.shape), const(conv.shape), const(wd.shape)],
        out_specs=tok,
        out_shape=jax.ShapeDtypeStruct((b, s, D_MODEL), _F32),
        scratch_shapes=[pltpu.VMEM((tile + 2 * SUBLANES, D_MODEL), _BF16),
                        pltpu.VMEM((tile, D_FF), _BF16)],
        compiler_params=_cparams(2),
        name="conv_ffn",
    )(x, x, x, mod, g, wu, conv, wd)


def _wg_head_perm():
    order = []
    for c in range(WG_HEADS // 2):
        order += [c, c + WG_HEADS // 2]
    return np.concatenate([np.arange(h * HEAD_DIM, (h + 1) * HEAD_DIM) for h in order])


def _rope_tables(seq):
    t = np.arange(seq)
    row = (t // GRID_W).astype(np.float64)
    col = (t % GRID_W).astype(np.float64)
    half = HEAD_DIM // 2
    n_freq = half // 2
    inv = ROPE_BASE ** (-np.arange(n_freq, dtype=np.float64) / n_freq)
    ang = np.concatenate([row[:, None] * inv, col[:, None] * inv], axis=-1)
    cos = np.concatenate([np.cos(ang), np.cos(ang)], axis=-1)
    sin = np.concatenate([-np.sin(ang), np.sin(ang)], axis=-1)
    reps = LANES // HEAD_DIM
    return (jnp.asarray(np.tile(cos, (1, reps)), _F32), jnp.asarray(np.tile(sin, (1, reps)), _F32))


def _na_bias_table(rpb, n_rows):
    wr = min(NA_WIN_R, n_rows)
    q = np.arange(GRID_W)
    c0 = np.clip(q - NA_WIN_C // 2, 0, GRID_W - NA_WIN_C)
    kc = np.arange(GRID_W)
    ok = (kc[None, :] >= c0[:, None]) & (kc[None, :] < c0[:, None] + NA_WIN_C)
    pad = GRID_W - NA_WIN_C
    padded = jnp.pad(rpb.astype(_F32), ((0, 0), (0, 0), (pad, pad)))
    toep = jnp.stack([padded[:, :, GRID_W - 1 - qi:2 * GRID_W - 1 - qi] for qi in range(GRID_W)],
                     axis=2)
    toep = jnp.where(ok[None, None], toep, NEG)
    tab = jnp.stack([toep[:, off:off + wr] for off in range(NA_WIN_R)], axis=0)
    tab = jnp.transpose(tab, (0, 1, 3, 2, 4))
    return tab.reshape(NA_WIN_R, NA_HEADS, GRID_W, wr * GRID_W)


def _block_diag_mean():
    idx = np.arange(MXU_DIM) // HEAD_DIM
    return jnp.asarray((idx[:, None] == idx[None, :]) / HEAD_DIM, _BF16)


def _pad_rows(a, rows):
    return jnp.concatenate([a, jnp.zeros((rows - a.shape[0],) + a.shape[1:], a.dtype)], axis=0)


def kernel(x, c, ctx, c_ctx, w_ada, b_ada, g_attn, w_in, qn_a, kn_a, qn_b, kn_b, rpb_a, sink_b,
           conv_c, w_o, g_ffn, w_up, conv_ffn, w_down):
    batch, seq, _ = x.shape
    c_len = ctx.shape[1]
    n_rows = seq // GRID_W
    scale = 1.0 / math.sqrt(HEAD_DIM)
    perm = _wg_head_perm()

    cvec = _pad_rows(jnp.concatenate([c, c_ctx[None, :]], axis=0), 2 * SUBLANES)
    mod = _ada_call(cvec, w_ada, b_ada)

    cos, sin = _rope_tables(seq)
    ones_c = jnp.ones((c_len, LANES), _F32)
    zeros_c = jnp.zeros((c_len, LANES), _F32)
    e = _block_diag_mean()

    xl, xc = x, ctx
    for l in range(DEPTH):
        update_ctx = l < DEPTH - 1
        mod_l = mod[l, :batch].reshape(batch, 6, D_MODEL)
        mod_c = jnp.broadcast_to(mod[l, batch].reshape(1, 6, D_MODEL), (batch, 6, D_MODEL))

        w_in_l = w_in[l]
        w_in_l = jnp.concatenate(
            [w_in_l[:, :_QB], w_in_l[:, _QB:_KB][:, perm], w_in_l[:, _KB:]], axis=1).astype(_BF16)
        w_o_l = w_o[l]
        w_o_l = jnp.concatenate(
            [w_o_l[:NA_W], w_o_l[NA_W:NA_W + WG_W][perm], w_o_l[NA_W + WG_W:]], axis=0).astype(_BF16)
        tile4 = lambda v: jnp.tile(v, MXU_DIM // HEAD_DIM)
        gains = _pad_rows(jnp.stack([tile4(qn_a[l]) * scale, tile4(kn_a[l]),
                                     tile4(qn_b[l]) * scale, tile4(kn_b[l])]), SUBLANES)
        bias = _na_bias_table(rpb_a[l], n_rows)
        conv_c_l = _pad_rows(conv_c[l], SUBLANES)
        g_attn_l = g_attn[l].reshape(1, D_MODEL)
        g_ffn_l = g_ffn[l].reshape(1, D_MODEL)
        wu = w_up[l].astype(_BF16)
        wd = w_down[l].astype(_BF16)
        taps = _pad_rows(conv_ffn[l], SUBLANES)
        sink_perm = sink_b[l]

        qa, ka, va, qb, kb, vb, cu, bg = _inproj_call(
            xl, mod_l, g_attn_l, w_in_l, gains, e, cos, sin, TOKEN_TILE)
        qa_c, ka_c, va_c, qb_c, kb_c, vb_c, cu_c, bg_c = _inproj_call(
            xc, mod_c, g_attn_l, w_in_l, gains, e, ones_c, zeros_c, c_len)
        ya = _na_call(qa, ka, va, ka_c, va_c, bias)
        yb = _wg_call(sink_perm, qb, kb, vb, kb_c, vb_c)
        xl = _outproj_call(ya, yb, cu, bg, xl, mod_l, conv_c_l, w_o_l, TOKEN_TILE)
        if update_ctx:
            ya_c, yb_c = _ctx_attn_call(sink_perm, qa_c, ka_c, va_c, qb_c, kb_c, vb_c)
            xc = _outproj_call(ya_c, yb_c, cu_c, bg_c, xc, mod_c, conv_c_l, w_o_l, c_len)

        xl = _ffn_call(xl, mod_l, g_ffn_l, wu, taps, wd, TOKEN_TILE)
        if update_ctx:
            xc = _ffn_call(xc, mod_c, g_ffn_l, wu, taps, wd, c_len)
    return xl
```

```python
import functools
import math

import jax
import jax.numpy as jnp
import numpy as np
from jax import lax
from jax.experimental import pallas as pl
from jax.experimental.pallas import tpu as pltpu

D_MODEL = 1024
DEPTH = 2
GRID_W = 64
HEAD_DIM = 64
NA_HEADS = 6
NA_WIN_R = 8
NA_WIN_C = 16
WG_HEADS = 6
WG_KV_HEADS = 2
WG_WINDOW = 128
WG_BLOCK = 128
SC_CH = 256
CONV_W = 3
D_FF = 2816
ROPE_BASE = 10000.0
EPS = 1e-6
NEG = -1e30
LOG2E = math.log2(math.e)

NA_W = NA_HEADS * HEAD_DIM
WG_W = WG_HEADS * HEAD_DIM
WG_KV_W = WG_KV_HEADS * HEAD_DIM
IN_W = 3 * NA_W + WG_W + 2 * WG_KV_W + 3 * SC_CH

LANES = 128
SUBLANES = 8
MXU_DIM = 256
FF_CHUNK = MXU_DIM
N_FF_CHUNKS = D_FF // FF_CHUNK
TOKEN_TILE = 512
INPROJ_SUB_TILE = 256
NA_ROWS_PER_STEP = 8
WG_BLOCKS_PER_STEP = 4
VMEM_LIMIT = 48 * 1024 * 1024

_QA, _KA, _VA = 0, NA_W, 2 * NA_W
_QB = 3 * NA_W
_KB = _QB + WG_W
_VB = _KB + WG_KV_W
_U = _VB + WG_KV_W
_BG = _U + SC_CH
_CG = _BG + SC_CH

_F32 = jnp.float32
_BF16 = jnp.bfloat16


def _dot(a, b):
    return jnp.dot(a, b, preferred_element_type=_F32)


def _dot_nt(a, b):
    return lax.dot_general(a, b, (((1,), (1,)), ((), ())), preferred_element_type=_F32)


def _cparams(n_grid):
    return pltpu.CompilerParams(
        dimension_semantics=("arbitrary",) * n_grid, vmem_limit_bytes=VMEM_LIMIT)


def _ada_kernel(c_ref, w_ref, b_ref, o_ref):
    c = c_ref[...]
    s = c / (1.0 + jnp.exp(-c))
    s_hi = s.astype(_BF16)
    s_lo = (s - s_hi.astype(_F32)).astype(_BF16)
    w = w_ref[...]
    w_hi = w.astype(_BF16)
    w_lo = (w - w_hi.astype(_F32)).astype(_BF16)
    o_ref[...] = _dot(s_hi, w_hi) + _dot(s_lo, w_hi) + _dot(s_hi, w_lo) + b_ref[...]


def _ada_call(cvec, w_ada, b_ada):
    rows = cvec.shape[0]
    n_out = w_ada.shape[-1]
    nt = 1536
    return pl.pallas_call(
        _ada_kernel,
        grid=(DEPTH, n_out // nt),
        in_specs=[
            pl.BlockSpec((rows, D_MODEL), lambda l, j: (0, 0)),
            pl.BlockSpec((None, D_MODEL, nt), lambda l, j: (l, 0, j)),
            pl.BlockSpec((None, 1, nt), lambda l, j: (l, 0, j)),
        ],
        out_specs=pl.BlockSpec((None, rows, nt), lambda l, j: (l, 0, j)),
        out_shape=jax.ShapeDtypeStruct((DEPTH, rows, n_out), _F32),
        compiler_params=_cparams(2),
        name="adaln_mod",
    )(cvec, w_ada, b_ada.reshape(DEPTH, 1, n_out))


def _norm_mod(x, g, shift, scale):
    ms = jnp.mean(x * x, axis=-1, keepdims=True)
    return (x * lax.rsqrt(ms + EPS) * g) * (1.0 + scale) + shift


def _head_norm(x, e, gain):
    sq = x * x
    hi = sq.astype(_BF16)
    lo = (sq - hi.astype(_F32)).astype(_BF16)
    ms = _dot(hi, e) + _dot(lo, e)
    return x * lax.rsqrt(ms + EPS) * gain


def _lane_id(shape):
    return lax.broadcasted_iota(jnp.int32, shape, len(shape) - 1)


def _inproj_kernel(x_ref, mod_ref, g_ref, w_ref, gain_ref, e_ref, cos_ref, sin_ref,
                   qa_ref, ka_ref, va_ref, qb_ref, kb_ref, vb_ref, cu_ref, bg_ref, p_scr, *, sub):
    e = e_ref[...]
    gain_qk_a = jnp.concatenate([gain_ref[0:1, :LANES], gain_ref[1:2, :LANES]], axis=1)
    gain_qk_b = jnp.concatenate([gain_ref[2:3, :LANES], gain_ref[3:4, :LANES]], axis=1)
    first_half = (_lane_id((sub, LANES)) % HEAD_DIM) < (HEAD_DIM // 2)
    for t in range(x_ref.shape[0] // sub):
        rows = slice(t * sub, (t + 1) * sub)
        h = _norm_mod(x_ref[rows, :], g_ref[...], mod_ref[0:1, :], mod_ref[1:2, :]).astype(_BF16)
        p_scr[rows, :] = _dot(h, w_ref[...])

        p = p_scr[rows, _QA:_QA + MXU_DIM]
        qa_ref[rows, 0:MXU_DIM] = _head_norm(p, e, gain_ref[0:1, :]).astype(_BF16)
        p = p_scr[rows, _QA + MXU_DIM:_QA + 2 * MXU_DIM]
        pn = _head_norm(p, e, gain_qk_a).astype(_BF16)
        qa_ref[rows, MXU_DIM:NA_W] = pn[:, :LANES]
        ka_ref[rows, 0:LANES] = pn[:, LANES:]
        p = p_scr[rows, _QA + 2 * MXU_DIM:_QA + 3 * MXU_DIM]
        ka_ref[rows, LANES:NA_W] = _head_norm(p, e, gain_ref[1:2, :]).astype(_BF16)

        va_ref[rows, :] = p_scr[rows, _VA:_VA + NA_W].astype(_BF16)

        cos = cos_ref[rows, :]
        sin = sin_ref[rows, :]
        for k in range(2):
            p = p_scr[rows, _QB + k * MXU_DIM:_QB + (k + 1) * MXU_DIM]
            pn = _head_norm(p, e, gain_ref[2:3, :] if k == 0 else gain_qk_b)
            for j in range(2):
                xc = pn[:, j * LANES:(j + 1) * LANES]
                swapped = jnp.where(first_half, pltpu.roll(xc, LANES - HEAD_DIM // 2, 1),
                                    pltpu.roll(xc, HEAD_DIM // 2, 1))
                xr = (xc * cos + swapped * sin).astype(_BF16)
                col = k * MXU_DIM + j * LANES
                if col < WG_W:
                    qb_ref[rows, col:col + LANES] = xr
                else:
                    kb_ref[rows, :] = xr

        vb_ref[rows, :] = p_scr[rows, _VB:_VB + WG_KV_W].astype(_BF16)
        bg_ref[rows, :] = p_scr[rows, _BG:_BG + SC_CH]
        cu_ref[rows, :] = p_scr[rows, _CG:_CG + SC_CH] * p_scr[rows, _U:_U + SC_CH]


def _inproj_call(x, mod, g, w_in, gains, e, cos, sin, tile):
    b, s, _ = x.shape
    nt = s // tile
    tok = lambda width: pl.BlockSpec((None, tile, width), lambda bi, i: (bi, i, 0))
    const = lambda shape: pl.BlockSpec(shape, lambda bi, i: (0,) * len(shape))
    widths = (NA_W, NA_W, NA_W, WG_W, WG_KV_W, WG_KV_W, SC_CH, SC_CH)
    dtypes = (_BF16,) * 6 + (_F32, _F32)
    return pl.pallas_call(
        functools.partial(_inproj_kernel, sub=min(tile, INPROJ_SUB_TILE)),
        grid=(b, nt),
        in_specs=[
            tok(D_MODEL),
            pl.BlockSpec((None, 6, D_MODEL), lambda bi, i: (bi, 0, 0)),
            const((1, D_MODEL)),
            const((D_MODEL, IN_W)),
            const((SUBLANES, MXU_DIM)),
            const((MXU_DIM, MXU_DIM)),
            pl.BlockSpec((tile, LANES), lambda bi, i: (i, 0)),
            pl.BlockSpec((tile, LANES), lambda bi, i: (i, 0)),
        ],
        out_specs=[tok(w) for w in widths],
        out_shape=[jax.ShapeDtypeStruct((b, s, w), d) for w, d in zip(widths, dtypes)],
        scratch_shapes=[pltpu.VMEM((tile, IN_W), _F32)],
        compiler_params=_cparams(2),
        name="mixer_inproj",
    )(x, mod, g, w_in, gains, e, cos, sin)


def _masked_halves(qp):
    qf = qp.astype(_F32)
    low = _lane_id(qf.shape) < HEAD_DIM
    return (jnp.where(low, qf, 0.0).astype(_BF16), jnp.where(low, 0.0, qf).astype(_BF16))


def _softmax_parts(parts, extra=None):
    m = parts[0].max(axis=-1, keepdims=True)
    for s in parts[1:]:
        m = jnp.maximum(m, s.max(axis=-1, keepdims=True))
    if extra is not None:
        m = jnp.maximum(m, extra)
    ps = [jnp.exp2(s - m) for s in parts]
    denom = ps[0].sum(axis=-1, keepdims=True)
    for p in ps[1:]:
        denom = denom + p.sum(axis=-1, keepdims=True)
    if extra is not None:
        denom = denom + jnp.exp2(extra - m)
    return [p.astype(_BF16) for p in ps], denom


def _na_kernel(q_ref, k_ref, v_ref, kc_ref, vc_ref, bias_ref, o_ref, *, rows_per_step, n_rows):
    step = pl.program_id(1)
    wr = min(NA_WIN_R, n_rows)
    low = _lane_id((GRID_W, LANES)) < HEAD_DIM
    offs, kstarts = [], []
    for rr in range(rows_per_step):
        r = step * rows_per_step + rr
        r0 = jnp.clip(r - wr // 2, 0, n_rows - wr)
        offs.append(r0 - r + NA_WIN_R - 1)
        kstarts.append(pl.multiple_of(r0 * GRID_W, GRID_W))

    for c in range(NA_HEADS // 2):
        cols = slice(c * LANES, (c + 1) * LANES)
        lhs = jnp.concatenate(
            [half for rr in range(rows_per_step)
             for half in _masked_halves(q_ref[rr * GRID_W:(rr + 1) * GRID_W, cols])], axis=0)
        s_ctx = _dot_nt(lhs, kc_ref[:, cols])
        p_ctx, o_loc, denoms = [], [], []
        for rr in range(rows_per_step):
            rows = slice(rr * 2 * GRID_W, (rr + 1) * 2 * GRID_W)
            kl = k_ref[pl.ds(kstarts[rr], wr * GRID_W), cols]
            vl = v_ref[pl.ds(kstarts[rr], wr * GRID_W), cols]
            bias = jnp.concatenate(
                [bias_ref[offs[rr], 2 * c], bias_ref[offs[rr], 2 * c + 1]], axis=0)
            s_loc = _dot_nt(lhs[rows], kl) + bias
            (pl_, pc_), dn = _softmax_parts([s_loc, s_ctx[rows]])
            o_loc.append(_dot(pl_, vl))
            p_ctx.append(pc_)
            denoms.append(dn)
        o_ctx = _dot(jnp.concatenate(p_ctx, axis=0), vc_ref[:, cols])
        for rr in range(rows_per_step):
            rows = slice(rr * 2 * GRID_W, (rr + 1) * 2 * GRID_W)
            o = (o_loc[rr] + o_ctx[rows]) / denoms[rr]
            o_ref[rr * GRID_W:(rr + 1) * GRID_W, cols] = jnp.where(
                low, o[:GRID_W], o[GRID_W:]).astype(_BF16)


def _na_call(qa, ka, va, kc, vc, bias):
    b, s, _ = qa.shape
    n_rows = s // GRID_W
    c_len = kc.shape[1]
    rps = NA_ROWS_PER_STEP
    blk = rps * GRID_W
    return pl.pallas_call(
        functools.partial(_na_kernel, rows_per_step=rps, n_rows=n_rows),
        grid=(b, n_rows // rps),
        in_specs=[
            pl.BlockSpec((None, blk, NA_W), lambda bi, j: (bi, j, 0)),
            pl.BlockSpec((None, s, NA_W), lambda bi, j: (bi, 0, 0)),
            pl.BlockSpec((None, s, NA_W), lambda bi, j: (bi, 0, 0)),
            pl.BlockSpec((None, c_len, NA_W), lambda bi, j: (bi, 0, 0)),
            pl.BlockSpec((None, c_len, NA_W), lambda bi, j: (bi, 0, 0)),
            pl.BlockSpec(bias.shape, lambda bi, j: (0, 0, 0, 0)),
        ],
        out_specs=pl.BlockSpec((None, blk, NA_W), lambda bi, j: (bi, j, 0)),
        out_shape=jax.ShapeDtypeStruct((b, s, NA_W), _BF16),
        compiler_params=_cparams(2),
        name="neighbourhood_attn",
    )(qa, ka, va, kc, vc, bias)


def _wg_window_start(blk, seq):
    span = 3 * WG_BLOCK
    return pl.multiple_of(jnp.clip((blk - 1) * WG_BLOCK, 0, seq - span), WG_BLOCK)


def _wg_scores(q_ref, k_ref, kc_ref, step, s_loc_scr, s_ctx_scr, *, seq, n_blk):
    span = 3 * WG_BLOCK
    per_blk = WG_HEADS * WG_BLOCK
    pieces = []
    for bb in range(n_blk):
        for c in range(WG_HEADS // 2):
            pieces.extend(_masked_halves(
                q_ref[bb * WG_BLOCK:(bb + 1) * WG_BLOCK, c * LANES:(c + 1) * LANES]))
    lhs = jnp.concatenate(pieces, axis=0)
    s_ctx_scr[...] = _dot_nt(lhs, kc_ref[...])
    for bb in range(n_blk):
        start = _wg_window_start(step * n_blk + bb, seq)
        s_loc_scr[bb * per_blk:(bb + 1) * per_blk, :] = _dot_nt(
            lhs[bb * per_blk:(bb + 1) * per_blk], k_ref[pl.ds(start, span), :])


def _wg_softmax_pv(sink_ref, s_loc_scr, s_ctx_scr, v_ref, vc_ref, step, o_ref, *, seq, n_blk):
    span = 3 * WG_BLOCK
    per_blk = WG_HEADS * WG_BLOCK
    low = _lane_id((WG_BLOCK, LANES)) < HEAD_DIM
    row_id = lax.broadcasted_iota(jnp.int32, (WG_BLOCK, span), 0)
    col_id = lax.broadcasted_iota(jnp.int32, (WG_BLOCK, span), 1)
    p_ctx, o_loc, denoms = [], [], []
    for bb in range(n_blk):
        blk = step * n_blk + bb
        start = _wg_window_start(blk, seq)
        dist = (blk * WG_BLOCK - start) + row_id - col_id
        mask_add = jnp.where(jnp.abs(dist) <= WG_WINDOW, 0.0, NEG).astype(_F32)
        p_loc = []
        for hb in range(WG_HEADS):
            head = (hb // 2) + (WG_HEADS // 2) * (hb % 2)
            rows = slice(bb * per_blk + hb * WG_BLOCK, bb * per_blk + (hb + 1) * WG_BLOCK)
            (pl_, pc_), dn = _softmax_parts(
                [s_loc_scr[rows, :] + mask_add, s_ctx_scr[rows, :]], sink_ref[head])
            p_loc.append(pl_)
            p_ctx.append(pc_)
            denoms.append(dn)
        o_loc.append(_dot(jnp.concatenate(p_loc, axis=0), v_ref[pl.ds(start, span), :]))
    o_ctx = _dot(jnp.concatenate(p_ctx, axis=0), vc_ref[...])
    for bb in range(n_blk):
        for c in range(WG_HEADS // 2):
            halves = []
            for half in range(2):
                hb = 2 * c + half
                rows = slice(hb * WG_BLOCK, (hb + 1) * WG_BLOCK)
                rows_all = slice(bb * per_blk + hb * WG_BLOCK, bb * per_blk + (hb + 1) * WG_BLOCK)
                halves.append((o_loc[bb][rows] + o_ctx[rows_all]) / denoms[bb * WG_HEADS + hb])
            o_ref[bb * WG_BLOCK:(bb + 1) * WG_BLOCK, c * LANES:(c + 1) * LANES] = jnp.where(
                low, halves[0], halves[1]).astype(_BF16)


def _wg_kernel(sink_ref, q_ref, qn_ref, k_ref, kn_ref, v_ref, kc_ref, kcn_ref, vc_ref, o_ref,
               sa_loc, sa_ctx, sb_loc, sb_ctx, *, seq, n_blk):
    bi, step = pl.program_id(0), pl.program_id(1)
    nxt = (step + 1) % pl.num_programs(1)
    scores = functools.partial(_wg_scores, seq=seq, n_blk=n_blk)
    finish = functools.partial(_wg_softmax_pv, sink_ref, seq=seq, n_blk=n_blk)

    @pl.when((bi == 0) & (step == 0))
    def _():
        scores(q_ref, k_ref, kc_ref, step, sa_loc, sa_ctx)

    @pl.when(step % 2 == 0)
    def _():
        scores(qn_ref, kn_ref, kcn_ref, nxt, sb_loc, sb_ctx)
        finish(sa_loc, sa_ctx, v_ref, vc_ref, step, o_ref)

    @pl.when(step % 2 == 1)
    def _():
        scores(qn_ref, kn_ref, kcn_ref, nxt, sa_loc, sa_ctx)
        finish(sb_loc, sb_ctx, v_ref, vc_ref, step, o_ref)


def _next_step_maps(n_batch, n_steps):
    nb = lambda bi, i: jnp.minimum(bi + (i + 1) // n_steps, n_batch - 1)
    return (lambda bi, i: (nb(bi, i), (i + 1) % n_steps, 0)), (lambda bi, i: (nb(bi, i), 0, 0))


def _wg_call(sink, qb, kb, vb, kc, vc):
    b, s, _ = qb.shape
    c_len = kc.shape[1]
    n_blk = WG_BLOCKS_PER_STEP
    q_rows = n_blk * WG_BLOCK
    n_steps = s // q_rows
    assert n_steps % 2 == 0
    next_q, next_kv = _next_step_maps(b, n_steps)
    cur_q = lambda bi, i: (bi, i, 0)
    cur_kv = lambda bi, i: (bi, 0, 0)
    s_rows = n_blk * WG_HEADS * WG_BLOCK
    return pl.pallas_call(
        functools.partial(_wg_kernel, seq=s, n_blk=n_blk),
        grid=(b, n_steps),
        in_specs=[
            pl.BlockSpec(memory_space=pltpu.SMEM),
            pl.BlockSpec((None, q_rows, WG_W), cur_q),
            pl.BlockSpec((None, q_rows, WG_W), next_q),
            pl.BlockSpec((None, s, WG_KV_W), cur_kv),
            pl.BlockSpec((None, s, WG_KV_W), next_kv),
            pl.BlockSpec((None, s, WG_KV_W), cur_kv),
            pl.BlockSpec((None, c_len, WG_KV_W), cur_kv),
            pl.BlockSpec((None, c_len, WG_KV_W), next_kv),
            pl.BlockSpec((None, c_len, WG_KV_W), cur_kv),
        ],
        out_specs=pl.BlockSpec((None, q_rows, WG_W), cur_q),
        out_shape=jax.ShapeDtypeStruct((b, s, WG_W), _BF16),
        scratch_shapes=[pltpu.VMEM((s_rows, 3 * WG_BLOCK), _F32), pltpu.VMEM((s_rows, c_len), _F32),
                        pltpu.VMEM((s_rows, 3 * WG_BLOCK), _F32), pltpu.VMEM((s_rows, c_len), _F32)],
        compiler_params=_cparams(2),
        name="window_gqa",
    )(sink, qb, qb, kb, kb, vb, kc, kc, vc)


def _ctx_attn_kernel(sink_ref, qa_ref, ka_ref, va_ref, qb_ref, kb_ref, vb_ref, ya_ref, yb_ref):
    c_len = qa_ref.shape[0]
    low = _lane_id((c_len, LANES)) < HEAD_DIM
    for c in range(NA_HEADS // 2):
        cols = slice(c * LANES, (c + 1) * LANES)
        lhs = jnp.concatenate(_masked_halves(qa_ref[:, cols]), axis=0)
        s = _dot_nt(lhs, ka_ref[:, cols])
        (p,), denom = _softmax_parts([s])
        o = _dot(p, va_ref[:, cols]) / denom
        ya_ref[:, cols] = jnp.where(low, o[:c_len], o[c_len:]).astype(_BF16)
    kb = kb_ref[...]
    vb = vb_ref[...]
    for c in range(WG_HEADS // 2):
        cols = slice(c * LANES, (c + 1) * LANES)
        halves = _masked_halves(qb_ref[:, cols])
        outs = []
        for half in range(2):
            head = c + (WG_HEADS // 2) * half
            s = _dot_nt(halves[half], kb)
            (p,), denom = _softmax_parts([s], sink_ref[head])
            outs.append(_dot(p, vb) / denom)
        yb_ref[:, cols] = jnp.where(low, outs[0], outs[1]).astype(_BF16)


def _ctx_attn_call(sink, qa, ka, va, qb, kb, vb):
    b, c_len, _ = qa.shape
    spec = lambda w: pl.BlockSpec((None, c_len, w), lambda bi: (bi, 0, 0))
    return pl.pallas_call(
        _ctx_attn_kernel,
        grid=(b,),
        in_specs=[pl.BlockSpec(memory_space=pltpu.SMEM), spec(NA_W), spec(NA_W), spec(NA_W),
                  spec(WG_W), spec(WG_KV_W), spec(WG_KV_W)],
        out_specs=[spec(NA_W), spec(WG_W)],
        out_shape=[jax.ShapeDtypeStruct((b, c_len, NA_W), _BF16),
                   jax.ShapeDtypeStruct((b, c_len, WG_W), _BF16)],
        compiler_params=_cparams(1),
        name="context_attn",
    )(sink, qa, ka, va, qb, kb, vb)


def _shift_rows(ext, tile):
    n = ext.shape[0]
    prev = pltpu.roll(ext, 1, 0)[SUBLANES:SUBLANES + tile]
    nxt = pltpu.roll(ext, n - 1, 0)[SUBLANES:SUBLANES + tile]
    return prev, nxt


def _outproj_kernel(ya_ref, yb_ref, cu_ref, cu_prev_ref, cu_next_ref, bg_ref, x_ref, mod_ref,
                    conv_ref, wo_ref, o_ref):
    i = pl.program_id(1)
    tile = cu_ref.shape[0]
    cu = cu_ref[...]
    prev_ok = (i > 0).astype(_F32)
    next_ok = (i < pl.num_programs(1) - 1).astype(_F32)
    ext = jnp.concatenate([cu_prev_ref[...] * prev_ok, cu, cu_next_ref[...] * next_ok], axis=0)
    prev, nxt = _shift_rows(ext, tile)
    yc = bg_ref[...] * (prev * conv_ref[0:1, :] + cu * conv_ref[1:2, :] + nxt * conv_ref[2:3, :])
    y = (_dot(ya_ref[...], wo_ref[0:NA_W, :])
         + _dot(yb_ref[...], wo_ref[NA_W:NA_W + WG_W, :])
         + _dot(yc.astype(_BF16), wo_ref[NA_W + WG_W:, :]))
    o_ref[...] = x_ref[...] + mod_ref[2:3, :] * y


def _halo_specs(tile, width, n_rows):
    per = tile // SUBLANES
    last = n_rows // SUBLANES - 1
    prev = pl.BlockSpec((None, SUBLANES, width),
                        lambda bi, i: (bi, jnp.maximum(i * per - 1, 0), 0))
    nxt = pl.BlockSpec((None, SUBLANES, width),
                       lambda bi, i: (bi, jnp.minimum((i + 1) * per, last), 0))
    return prev, nxt


def _outproj_call(ya, yb, cu, bg, x, mod, conv_c, w_o, tile):
    b, s, _ = x.shape
    tok = lambda width: pl.BlockSpec((None, tile, width), lambda bi, i: (bi, i, 0))
    const = lambda shape: pl.BlockSpec(shape, lambda bi, i: (0,) * len(shape))
    prev, nxt = _halo_specs(tile, SC_CH, s)
    return pl.pallas_call(
        _outproj_kernel,
        grid=(b, s // tile),
        in_specs=[tok(NA_W), tok(WG_W), tok(SC_CH), prev, nxt, tok(SC_CH), tok(D_MODEL),
                  pl.BlockSpec((None, 6, D_MODEL), lambda bi, i: (bi, 0, 0)),
                  const((SUBLANES, SC_CH)), const((D_MODEL, D_MODEL))],
        out_specs=tok(D_MODEL),
        out_shape=jax.ShapeDtypeStruct((b, s, D_MODEL), _F32),
        compiler_params=_cparams(2),
        name="mixer_outproj",
    )(ya, yb, cu, cu, cu, bg, x, mod, conv_c, w_o)


def _ffn_kernel(x_ref, xp_ref, xn_ref, mod_ref, g_ref, wu_ref, conv_ref, wd_ref, o_ref,
                h_scr, act_scr):
    i = pl.program_id(1)
    tile = x_ref.shape[0]
    g = g_ref[...]
    shift, scale = mod_ref[3:4, :], mod_ref[4:5, :]
    prev_ok = (i > 0).astype(_F32)
    next_ok = (i < pl.num_programs(1) - 1).astype(_F32)
    h_scr[...] = jnp.concatenate(
        [_norm_mod(xp_ref[...], g, shift, scale) * prev_ok,
         _norm_mod(x_ref[...], g, shift, scale),
         _norm_mod(xn_ref[...], g, shift, scale) * next_ok], axis=0).astype(_BF16)

    mid = slice(SUBLANES, SUBLANES + tile)
    for c in range(N_FF_CHUNKS):
        a_cols = slice(c * FF_CHUNK, (c + 1) * FF_CHUNK)
        g_cols = slice(D_FF + c * FF_CHUNK, D_FF + (c + 1) * FF_CHUNK)
        h = h_scr[...]
        ua = _dot(h, wu_ref[:, a_cols])
        ug = _dot(h, wu_ref[:, g_cols])
        pa, na = _shift_rows(ua, tile)
        pg, ng = _shift_rows(ug, tile)
        a = (pa * conv_ref[0:1, a_cols] + ua[mid] * conv_ref[1:2, a_cols]
             + na * conv_ref[2:3, a_cols])
        gt = (pg * conv_ref[0:1, g_cols] + ug[mid] * conv_ref[1:2, g_cols]
              + ng * conv_ref[2:3, g_cols])
        act_scr[:, a_cols] = ((a / (1.0 + jnp.exp(-a))) * gt).astype(_BF16)

    o_ref[...] = x_ref[...] + mod_ref[5:6, :] * _dot(act_scr[...], wd_ref[...])


def _ffn_call(x, mod, g, wu, conv, wd, tile):
    b, s, _ = x.shape
    tok = pl.BlockSpec((None, tile, D_MODEL), lambda bi, i: (bi, i, 0))
    const = lambda shape: pl.BlockSpec(shape, lambda bi, i: (0,) * len(shape),
                                       pipeline_mode=pl.Buffered(1))
    prev, nxt = _halo_specs(tile, D_MODEL, s)
    return pl.pallas_call(
        _ffn_kernel,
        grid=(b, s // tile),
        in_specs=[tok, prev, nxt,
                  pl.BlockSpec((None, 6, D_MODEL), lambda bi, i: (bi, 0, 0)),
                  pl.BlockSpec((1, D_MODEL), lambda bi, i: (0, 0)),
                  const(wu.shape), const(conv.shape), const(wd.shape)],
        out_specs=tok,
        out_shape=jax.ShapeDtypeStruct((b, s, D_MODEL), _F32),
        scratch_shapes=[pltpu.VMEM((tile + 2 * SUBLANES, D_MODEL), _BF16),
                        pltpu.VMEM((tile, D_FF), _BF16)],
        compiler_params=_cparams(2),
        name="conv_ffn",
    )(x, x, x, mod, g, wu, conv, wd)


def _wg_head_perm():
    order = []
    for c in range(WG_HEADS // 2):
        order += [c, c + WG_HEADS // 2]
    return np.concatenate([np.arange(h * HEAD_DIM, (h + 1) * HEAD_DIM) for h in order])


def _rope_tables(seq):
    t = np.arange(seq)
    row = (t // GRID_W).astype(np.float64)
    col = (t % GRID_W).astype(np.float64)
    half = HEAD_DIM // 2
    n_freq = half // 2
    inv = ROPE_BASE ** (-np.arange(n_freq, dtype=np.float64) / n_freq)
    ang = np.concatenate([row[:, None] * inv, col[:, None] * inv], axis=-1)
    cos = np.concatenate([np.cos(ang), np.cos(ang)], axis=-1)
    sin = np.concatenate([-np.sin(ang), np.sin(ang)], axis=-1)
    reps = LANES // HEAD_DIM
    return (jnp.asarray(np.tile(cos, (1, reps)), _F32), jnp.asarray(np.tile(sin, (1, reps)), _F32))


def _na_bias_table(rpb, n_rows):
    wr = min(NA_WIN_R, n_rows)
    q = np.arange(GRID_W)
    c0 = np.clip(q - NA_WIN_C // 2, 0, GRID_W - NA_WIN_C)
    kc = np.arange(GRID_W)
    ok = (kc[None, :] >= c0[:, None]) & (kc[None, :] < c0[:, None] + NA_WIN_C)
    pad = GRID_W - NA_WIN_C
    padded = jnp.pad(rpb.astype(_F32), ((0, 0), (0, 0), (pad, pad)))
    toep = jnp.stack([padded[:, :, GRID_W - 1 - qi:2 * GRID_W - 1 - qi] for qi in range(GRID_W)],
                     axis=1)
    toep = jnp.where(ok[None, :, None, :], toep * LOG2E, NEG)
    n_d = 2 * NA_WIN_R - 1
    toep = toep.reshape(NA_HEADS, GRID_W, n_d * GRID_W)
    return jnp.stack([toep[:, :, off * GRID_W:(off + wr) * GRID_W] for off in range(NA_WIN_R)],
                     axis=0)


def _block_diag_mean():
    idx = np.arange(MXU_DIM) // HEAD_DIM
    return jnp.asarray((idx[:, None] == idx[None, :]) / HEAD_DIM, _BF16)


def _pad_rows(a, rows):
    return jnp.concatenate([a, jnp.zeros((rows - a.shape[0],) + a.shape[1:], a.dtype)], axis=0)


def kernel(x, c, ctx, c_ctx, w_ada, b_ada, g_attn, w_in, qn_a, kn_a, qn_b, kn_b, rpb_a, sink_b,
           conv_c, w_o, g_ffn, w_up, conv_ffn, w_down):
    batch, seq, _ = x.shape
    c_len = ctx.shape[1]
    n_rows = seq // GRID_W
    scale = LOG2E / math.sqrt(HEAD_DIM)
    perm = _wg_head_perm()

    cvec = _pad_rows(jnp.concatenate([c, c_ctx[None, :]], axis=0), 2 * SUBLANES)
    mod = _ada_call(cvec, w_ada, b_ada)

    cos, sin = _rope_tables(seq)
    ones_c = jnp.ones((c_len, LANES), _F32)
    zeros_c = jnp.zeros((c_len, LANES), _F32)
    e = _block_diag_mean()

    xl, xc = x, ctx
    for l in range(DEPTH):
        update_ctx = l < DEPTH - 1
        mod_l = mod[l, :batch].reshape(batch, 6, D_MODEL)
        mod_c = jnp.broadcast_to(mod[l, batch].reshape(1, 6, D_MODEL), (batch, 6, D_MODEL))

        w_in_l = w_in[l]
        w_in_l = jnp.concatenate(
            [w_in_l[:, :_QB], w_in_l[:, _QB:_KB][:, perm], w_in_l[:, _KB:]], axis=1).astype(_BF16)
        w_o_l = w_o[l]
        w_o_l = jnp.concatenate(
            [w_o_l[:NA_W], w_o_l[NA_W:NA_W + WG_W][perm], w_o_l[NA_W + WG_W:]], axis=0).astype(_BF16)
        tile4 = lambda v: jnp.tile(v, MXU_DIM // HEAD_DIM)
        gains = _pad_rows(jnp.stack([tile4(qn_a[l]) * scale, tile4(kn_a[l]),
                                     tile4(qn_b[l]) * scale, tile4(kn_b[l])]), SUBLANES)
        bias = _na_bias_table(rpb_a[l], n_rows)
        conv_c_l = _pad_rows(conv_c[l], SUBLANES)
        g_attn_l = g_attn[l].reshape(1, D_MODEL)
        g_ffn_l = g_ffn[l].reshape(1, D_MODEL)
        wu = w_up[l].astype(_BF16)
        wd = w_down[l].astype(_BF16)
        taps = _pad_rows(conv_ffn[l], SUBLANES)
        sink_perm = sink_b[l] * LOG2E

        qa, ka, va, qb, kb, vb, cu, bg = _inproj_call(
            xl, mod_l, g_attn_l, w_in_l, gains, e, cos, sin, TOKEN_TILE)
        qa_c, ka_c, va_c, qb_c, kb_c, vb_c, cu_c, bg_c = _inproj_call(
            xc, mod_c, g_attn_l, w_in_l, gains, e, ones_c, zeros_c, c_len)
        ya = _na_call(qa, ka, va, ka_c, va_c, bias)
        yb = _wg_call(sink_perm, qb, kb, vb, kb_c, vb_c)
        xl = _outproj_call(ya, yb, cu, bg, xl, mod_l, conv_c_l, w_o_l, TOKEN_TILE)
        if update_ctx:
            ya_c, yb_c = _ctx_attn_call(sink_perm, qa_c, ka_c, va_c, qb_c, kb_c, vb_c)
            xc = _outproj_call(ya_c, yb_c, cu_c, bg_c, xc, mod_c, conv_c_l, w_o_l, c_len)

        xl = _ffn_call(xl, mod_l, g_ffn_l, wu, taps, wd, TOKEN_TILE)
        if update_ctx:
            xc = _ffn_call(xc, mod_c, g_ffn_l, wu, taps, wd, c_len)
    return xl
```

```python
import functools
import math

import jax
import jax.numpy as jnp
import numpy as np
from jax import lax
from jax.experimental import pallas as pl
from jax.experimental.pallas import tpu as pltpu

D_MODEL = 1024
DEPTH = 2
GRID_W = 64
HEAD_DIM = 64
NA_HEADS = 6
NA_WIN_R = 8
NA_WIN_C = 16
WG_HEADS = 6
WG_KV_HEADS = 2
WG_WINDOW = 128
WG_BLOCK = 128
SC_CH = 256
CONV_W = 3
D_FF = 2816
ROPE_BASE = 10000.0
EPS = 1e-6
NEG = -1e30
LOG2E = math.log2(math.e)

NA_W = NA_HEADS * HEAD_DIM
WG_W = WG_HEADS * HEAD_DIM
WG_KV_W = WG_KV_HEADS * HEAD_DIM
IN_W = 3 * NA_W + WG_W + 2 * WG_KV_W + 3 * SC_CH

LANES = 128
SUBLANES = 8
MXU_DIM = 256
FF_CHUNK = MXU_DIM
N_FF_CHUNKS = D_FF // FF_CHUNK
TOKEN_TILE = 512
INPROJ_SUB_TILE = 256
NA_ROWS_PER_STEP = 8
WG_BLOCKS_PER_STEP = 4
VMEM_LIMIT = 48 * 1024 * 1024

_QA, _KA, _VA = 0, NA_W, 2 * NA_W
_QB = 3 * NA_W
_KB = _QB + WG_W
_VB = _KB + WG_KV_W
_U = _VB + WG_KV_W
_BG = _U + SC_CH
_CG = _BG + SC_CH

_F32 = jnp.float32
_BF16 = jnp.bfloat16


def _dot(a, b):
    return jnp.dot(a, b, preferred_element_type=_F32)


def _dot_nt(a, b):
    return lax.dot_general(a, b, (((1,), (1,)), ((), ())), preferred_element_type=_F32)


def _cparams(n_grid):
    return pltpu.CompilerParams(
        dimension_semantics=("arbitrary",) * n_grid, vmem_limit_bytes=VMEM_LIMIT)


def _ada_kernel(c_ref, w_ref, b_ref, o_ref):
    c = c_ref[...]
    s = c / (1.0 + jnp.exp(-c))
    s_hi = s.astype(_BF16)
    s_lo = (s - s_hi.astype(_F32)).astype(_BF16)
    w = w_ref[...]
    w_hi = w.astype(_BF16)
    w_lo = (w - w_hi.astype(_F32)).astype(_BF16)
    o_ref[...] = _dot(s_hi, w_hi) + _dot(s_lo, w_hi) + _dot(s_hi, w_lo) + b_ref[...]


def _ada_call(cvec, w_ada, b_ada):
    rows = cvec.shape[0]
    n_out = w_ada.shape[-1]
    nt = 1536
    return pl.pallas_call(
        _ada_kernel,
        grid=(DEPTH, n_out // nt),
        in_specs=[
            pl.BlockSpec((rows, D_MODEL), lambda l, j: (0, 0)),
            pl.BlockSpec((None, D_MODEL, nt), lambda l, j: (l, 0, j)),
            pl.BlockSpec((None, 1, nt), lambda l, j: (l, 0, j)),
        ],
        out_specs=pl.BlockSpec((None, rows, nt), lambda l, j: (l, 0, j)),
        out_shape=jax.ShapeDtypeStruct((DEPTH, rows, n_out), _F32),
        compiler_params=_cparams(2),
        name="adaln_mod",
    )(cvec, w_ada, b_ada.reshape(DEPTH, 1, n_out))


def _norm_mod(x, g, shift, scale):
    ms = jnp.mean(x * x, axis=-1, keepdims=True)
    return (x * lax.rsqrt(ms + EPS) * g) * (1.0 + scale) + shift


def _head_norm(x, e, gain):
    sq = x * x
    hi = sq.astype(_BF16)
    lo = (sq - hi.astype(_F32)).astype(_BF16)
    ms = _dot(hi, e) + _dot(lo, e)
    return x * lax.rsqrt(ms + EPS) * gain


def _lane_id(shape):
    return lax.broadcasted_iota(jnp.int32, shape, len(shape) - 1)


def _inproj_kernel(x_ref, mod_ref, g_ref, w_ref, gain_ref, e_ref, cos_ref, sin_ref,
                   qa_ref, ka_ref, va_ref, qb_ref, kb_ref, vb_ref, cu_ref, bg_ref, p_scr, *, sub):
    e = e_ref[...]
    gain_qk_a = jnp.concatenate([gain_ref[0:1, :LANES], gain_ref[1:2, :LANES]], axis=1)
    gain_qk_b = jnp.concatenate([gain_ref[2:3, :LANES], gain_ref[3:4, :LANES]], axis=1)
    first_half = (_lane_id((sub, LANES)) % HEAD_DIM) < (HEAD_DIM // 2)
    for t in range(x_ref.shape[0] // sub):
        rows = slice(t * sub, (t + 1) * sub)
        h = _norm_mod(x_ref[rows, :], g_ref[...], mod_ref[0:1, :], mod_ref[1:2, :]).astype(_BF16)
        p_scr[rows, :] = _dot(h, w_ref[...])

        p = p_scr[rows, _QA:_QA + MXU_DIM]
        qa_ref[rows, 0:MXU_DIM] = _head_norm(p, e, gain_ref[0:1, :]).astype(_BF16)
        p = p_scr[rows, _QA + MXU_DIM:_QA + 2 * MXU_DIM]
        pn = _head_norm(p, e, gain_qk_a).astype(_BF16)
        qa_ref[rows, MXU_DIM:NA_W] = pn[:, :LANES]
        ka_ref[rows, 0:LANES] = pn[:, LANES:]
        p = p_scr[rows, _QA + 2 * MXU_DIM:_QA + 3 * MXU_DIM]
        ka_ref[rows, LANES:NA_W] = _head_norm(p, e, gain_ref[1:2, :]).astype(_BF16)

        va_ref[rows, :] = p_scr[rows, _VA:_VA + NA_W].astype(_BF16)

        cos = cos_ref[rows, :]
        sin = sin_ref[rows, :]
        for k in range(2):
            p = p_scr[rows, _QB + k * MXU_DIM:_QB + (k + 1) * MXU_DIM]
            pn = _head_norm(p, e, gain_ref[2:3, :] if k == 0 else gain_qk_b)
            for j in range(2):
                xc = pn[:, j * LANES:(j + 1) * LANES]
                swapped = jnp.where(first_half, pltpu.roll(xc, LANES - HEAD_DIM // 2, 1),
                                    pltpu.roll(xc, HEAD_DIM // 2, 1))
                xr = (xc * cos + swapped * sin).astype(_BF16)
                col = k * MXU_DIM + j * LANES
                if col < WG_W:
                    qb_ref[rows, col:col + LANES] = xr
                else:
                    kb_ref[rows, :] = xr

        vb_ref[rows, :] = p_scr[rows, _VB:_VB + WG_KV_W].astype(_BF16)
        bg_ref[rows, :] = p_scr[rows, _BG:_BG + SC_CH]
        cu_ref[rows, :] = p_scr[rows, _CG:_CG + SC_CH] * p_scr[rows, _U:_U + SC_CH]


def _inproj_call(x, mod, g, w_in, gains, e, cos, sin, tile):
    b, s, _ = x.shape
    nt = s // tile
    tok = lambda width: pl.BlockSpec((None, tile, width), lambda bi, i: (bi, i, 0))
    const = lambda shape: pl.BlockSpec(shape, lambda bi, i: (0,) * len(shape))
    widths = (NA_W, NA_W, NA_W, WG_W, WG_KV_W, WG_KV_W, SC_CH, SC_CH)
    dtypes = (_BF16,) * 6 + (_F32, _F32)
    return pl.pallas_call(
        functools.partial(_inproj_kernel, sub=min(tile, INPROJ_SUB_TILE)),
        grid=(b, nt),
        in_specs=[
            tok(D_MODEL),
            pl.BlockSpec((None, 6, D_MODEL), lambda bi, i: (bi, 0, 0)),
            const((1, D_MODEL)),
            const((D_MODEL, IN_W)),
            const((SUBLANES, MXU_DIM)),
            const((MXU_DIM, MXU_DIM)),
            pl.BlockSpec((tile, LANES), lambda bi, i: (i, 0)),
            pl.BlockSpec((tile, LANES), lambda bi, i: (i, 0)),
        ],
        out_specs=[tok(w) for w in widths],
        out_shape=[jax.ShapeDtypeStruct((b, s, w), d) for w, d in zip(widths, dtypes)],
        scratch_shapes=[pltpu.VMEM((tile, IN_W), _F32)],
        compiler_params=_cparams(2),
        name="mixer_inproj",
    )(x, mod, g, w_in, gains, e, cos, sin)


def _masked_halves(qp):
    qf = qp.astype(_F32)
    low = _lane_id(qf.shape) < HEAD_DIM
    return (jnp.where(low, qf, 0.0).astype(_BF16), jnp.where(low, 0.0, qf).astype(_BF16))


def _softmax_parts(parts, extra=None):
    m = parts[0].max(axis=-1, keepdims=True)
    for s in parts[1:]:
        m = jnp.maximum(m, s.max(axis=-1, keepdims=True))
    if extra is not None:
        m = jnp.maximum(m, extra)
    ps = [jnp.exp2(s - m) for s in parts]
    denom = ps[0].sum(axis=-1, keepdims=True)
    for p in ps[1:]:
        denom = denom + p.sum(axis=-1, keepdims=True)
    if extra is not None:
        denom = denom + jnp.exp2(extra - m)
    return [p.astype(_BF16) for p in ps], denom


def _na_kernel(q_ref, k_ref, v_ref, kc_ref, vc_ref, bias_ref, o_ref, *, rows_per_step, n_rows):
    step = pl.program_id(1)
    wr = min(NA_WIN_R, n_rows)
    low = _lane_id((GRID_W, LANES)) < HEAD_DIM
    offs, kstarts = [], []
    for rr in range(rows_per_step):
        r = step * rows_per_step + rr
        r0 = jnp.clip(r - wr // 2, 0, n_rows - wr)
        offs.append(r0 - r + NA_WIN_R - 1)
        kstarts.append(pl.multiple_of(r0 * GRID_W, GRID_W))

    for c in range(NA_HEADS // 2):
        cols = slice(c * LANES, (c + 1) * LANES)
        lhs = jnp.concatenate(
            [half for rr in range(rows_per_step)
             for half in _masked_halves(q_ref[rr * GRID_W:(rr + 1) * GRID_W, cols])], axis=0)
        s_ctx = _dot_nt(lhs, kc_ref[:, cols])
        p_ctx, o_loc, denoms = [], [], []
        for rr in range(rows_per_step):
            rows = slice(rr * 2 * GRID_W, (rr + 1) * 2 * GRID_W)
            kl = k_ref[pl.ds(kstarts[rr], wr * GRID_W), cols]
            vl = v_ref[pl.ds(kstarts[rr], wr * GRID_W), cols]
            bias = jnp.concatenate(
                [bias_ref[offs[rr], 2 * c], bias_ref[offs[rr], 2 * c + 1]], axis=0)
            s_loc = _dot_nt(lhs[rows], kl) + bias
            (pl_, pc_), dn = _softmax_parts([s_loc, s_ctx[rows]])
            o_loc.append(_dot(pl_, vl))
            p_ctx.append(pc_)
            denoms.append(dn)
        o_ctx = _dot(jnp.concatenate(p_ctx, axis=0), vc_ref[:, cols])
        for rr in range(rows_per_step):
            rows = slice(rr * 2 * GRID_W, (rr + 1) * 2 * GRID_W)
            o = (o_loc[rr] + o_ctx[rows]) / denoms[rr]
            o_ref[rr * GRID_W:(rr + 1) * GRID_W, cols] = jnp.where(
                low, o[:GRID_W], o[GRID_W:]).astype(_BF16)


def _na_call(qa, ka, va, kc, vc, bias):
    b, s, _ = qa.shape
    n_rows = s // GRID_W
    c_len = kc.shape[1]
    rps = NA_ROWS_PER_STEP
    blk = rps * GRID_W
    return pl.pallas_call(
        functools.partial(_na_kernel, rows_per_step=rps, n_rows=n_rows),
        grid=(b, n_rows // rps),
        in_specs=[
            pl.BlockSpec((None, blk, NA_W), lambda bi, j: (bi, j, 0)),
            pl.BlockSpec((None, s, NA_W), lambda bi, j: (bi, 0, 0)),
            pl.BlockSpec((None, s, NA_W), lambda bi, j: (bi, 0, 0)),
            pl.BlockSpec((None, c_len, NA_W), lambda bi, j: (bi, 0, 0)),
            pl.BlockSpec((None, c_len, NA_W), lambda bi, j: (bi, 0, 0)),
            pl.BlockSpec(bias.shape, lambda bi, j: (0, 0, 0, 0)),
        ],
        out_specs=pl.BlockSpec((None, blk, NA_W), lambda bi, j: (bi, j, 0)),
        out_shape=jax.ShapeDtypeStruct((b, s, NA_W), _BF16),
        compiler_params=_cparams(2),
        name="neighbourhood_attn",
    )(qa, ka, va, kc, vc, bias)


def _wg_window_start(blk, seq):
    span = 3 * WG_BLOCK
    return pl.multiple_of(jnp.clip((blk - 1) * WG_BLOCK, 0, seq - span), WG_BLOCK)


def _wg_scores(q_ref, k_ref, kc_ref, step, s_loc_scr, s_ctx_scr, *, seq, n_blk):
    span = 3 * WG_BLOCK
    per_blk = WG_HEADS * WG_BLOCK
    pieces = []
    for bb in range(n_blk):
        for c in range(WG_HEADS // 2):
            pieces.extend(_masked_halves(
                q_ref[bb * WG_BLOCK:(bb + 1) * WG_BLOCK, c * LANES:(c + 1) * LANES]))
    lhs = jnp.concatenate(pieces, axis=0)
    s_ctx_scr[...] = _dot_nt(lhs, kc_ref[...])
    for bb in range(n_blk):
        start = _wg_window_start(step * n_blk + bb, seq)
        s_loc_scr[bb * per_blk:(bb + 1) * per_blk, :] = _dot_nt(
            lhs[bb * per_blk:(bb + 1) * per_blk], k_ref[pl.ds(start, span), :])


def _wg_softmax_pv(sink_ref, s_loc_scr, s_ctx_scr, v_ref, vc_ref, step, o_ref, *, seq, n_blk):
    span = 3 * WG_BLOCK
    per_blk = WG_HEADS * WG_BLOCK
    low = _lane_id((WG_BLOCK, LANES)) < HEAD_DIM
    row_id = lax.broadcasted_iota(jnp.int32, (WG_BLOCK, span), 0)
    col_id = lax.broadcasted_iota(jnp.int32, (WG_BLOCK, span), 1)
    p_ctx, o_loc, denoms = [], [], []
    for bb in range(n_blk):
        blk = step * n_blk + bb
        start = _wg_window_start(blk, seq)
        dist = (blk * WG_BLOCK - start) + row_id - col_id
        mask_add = jnp.where(jnp.abs(dist) <= WG_WINDOW, 0.0, NEG).astype(_F32)
        p_loc = []
        for hb in range(WG_HEADS):
            head = (hb // 2) + (WG_HEADS // 2) * (hb % 2)
            rows = slice(bb * per_blk + hb * WG_BLOCK, bb * per_blk + (hb + 1) * WG_BLOCK)
            (pl_, pc_), dn = _softmax_parts(
                [s_loc_scr[rows, :] + mask_add, s_ctx_scr[rows, :]], sink_ref[head])
            p_loc.append(pl_)
            p_ctx.append(pc_)
            denoms.append(dn)
        o_loc.append(_dot(jnp.concatenate(p_loc, axis=0), v_ref[pl.ds(start, span), :]))
    o_ctx = _dot(jnp.concatenate(p_ctx, axis=0), vc_ref[...])
    for bb in range(n_blk):
        for c in range(WG_HEADS // 2):
            halves = []
            for half in range(2):
                hb = 2 * c + half
                rows = slice(hb * WG_BLOCK, (hb + 1) * WG_BLOCK)
                rows_all = slice(bb * per_blk + hb * WG_BLOCK, bb * per_blk + (hb + 1) * WG_BLOCK)
                halves.append((o_loc[bb][rows] + o_ctx[rows_all]) / denoms[bb * WG_HEADS + hb])
            o_ref[bb * WG_BLOCK:(bb + 1) * WG_BLOCK, c * LANES:(c + 1) * LANES] = jnp.where(
                low, halves[0], halves[1]).astype(_BF16)


def _wg_kernel(sink_ref, q_ref, qn_ref, k_ref, kn_ref, v_ref, kc_ref, kcn_ref, vc_ref, o_ref,
               sa_loc, sa_ctx, sb_loc, sb_ctx, *, seq, n_blk):
    bi, step = pl.program_id(0), pl.program_id(1)
    nxt = (step + 1) % pl.num_programs(1)
    scores = functools.partial(_wg_scores, seq=seq, n_blk=n_blk)
    finish = functools.partial(_wg_softmax_pv, sink_ref, seq=seq, n_blk=n_blk)

    @pl.when((bi == 0) & (step == 0))
    def _():
        scores(q_ref, k_ref, kc_ref, step, sa_loc, sa_ctx)

    @pl.when(step % 2 == 0)
    def _():
        scores(qn_ref, kn_ref, kcn_ref, nxt, sb_loc, sb_ctx)
        finish(sa_loc, sa_ctx, v_ref, vc_ref, step, o_ref)

    @pl.when(step % 2 == 1)
    def _():
        scores(qn_ref, kn_ref, kcn_ref, nxt, sa_loc, sa_ctx)
        finish(sb_loc, sb_ctx, v_ref, vc_ref, step, o_ref)


def _next_step_maps(n_batch, n_steps):
    nb = lambda bi, i: jnp.minimum(bi + (i + 1) // n_steps, n_batch - 1)
    return (lambda bi, i: (nb(bi, i), (i + 1) % n_steps, 0)), (lambda bi, i: (nb(bi, i), 0, 0))


def _wg_call(sink, qb, kb, vb, kc, vc):
    b, s, _ = qb.shape
    c_len = kc.shape[1]
    n_blk = WG_BLOCKS_PER_STEP
    q_rows = n_blk * WG_BLOCK
    n_steps = s // q_rows
    assert n_steps % 2 == 0
    next_q, next_kv = _next_step_maps(b, n_steps)
    cur_q = lambda bi, i: (bi, i, 0)
    cur_kv = lambda bi, i: (bi, 0, 0)
    s_rows = n_blk * WG_HEADS * WG_BLOCK
    return pl.pallas_call(
        functools.partial(_wg_kernel, seq=s, n_blk=n_blk),
        grid=(b, n_steps),
        in_specs=[
            pl.BlockSpec(memory_space=pltpu.SMEM),
            pl.BlockSpec((None, q_rows, WG_W), cur_q),
            pl.BlockSpec((None, q_rows, WG_W), next_q),
            pl.BlockSpec((None, s, WG_KV_W), cur_kv),
            pl.BlockSpec((None, s, WG_KV_W), next_kv),
            pl.BlockSpec((None, s, WG_KV_W), cur_kv),
            pl.BlockSpec((None, c_len, WG_KV_W), cur_kv),
            pl.BlockSpec((None, c_len, WG_KV_W), next_kv),
            pl.BlockSpec((None, c_len, WG_KV_W), cur_kv),
        ],
        out_specs=pl.BlockSpec((None, q_rows, WG_W), cur_q),
        out_shape=jax.ShapeDtypeStruct((b, s, WG_W), _BF16),
        scratch_shapes=[pltpu.VMEM((s_rows, 3 * WG_BLOCK), _F32), pltpu.VMEM((s_rows, c_len), _F32),
                        pltpu.VMEM((s_rows, 3 * WG_BLOCK), _F32), pltpu.VMEM((s_rows, c_len), _F32)],
        compiler_params=_cparams(2),
        name="window_gqa",
    )(sink, qb, qb, kb, kb, vb, kc, kc, vc)


def _ctx_attn_kernel(sink_ref, qa_ref, ka_ref, va_ref, qb_ref, kb_ref, vb_ref, ya_ref, yb_ref):
    c_len = qa_ref.shape[0]
    low = _lane_id((c_len, LANES)) < HEAD_DIM
    for c in range(NA_HEADS // 2):
        cols = slice(c * LANES, (c + 1) * LANES)
        lhs = jnp.concatenate(_masked_halves(qa_ref[:, cols]), axis=0)
        s = _dot_nt(lhs, ka_ref[:, cols])
        (p,), denom = _softmax_parts([s])
        o = _dot(p, va_ref[:, cols]) / denom
        ya_ref[:, cols] = jnp.where(low, o[:c_len], o[c_len:]).astype(_BF16)
    kb = kb_ref[...]
    vb = vb_ref[...]
    for c in range(WG_HEADS // 2):
        cols = slice(c * LANES, (c + 1) * LANES)
        halves = _masked_halves(qb_ref[:, cols])
        outs = []
        for half in range(2):
            head = c + (WG_HEADS // 2) * half
            s = _dot_nt(halves[half], kb)
            (p,), denom = _softmax_parts([s], sink_ref[head])
            outs.append(_dot(p, vb) / denom)
        yb_ref[:, cols] = jnp.where(low, outs[0], outs[1]).astype(_BF16)


def _ctx_attn_call(sink, qa, ka, va, qb, kb, vb):
    b, c_len, _ = qa.shape
    spec = lambda w: pl.BlockSpec((None, c_len, w), lambda bi: (bi, 0, 0))
    return pl.pallas_call(
        _ctx_attn_kernel,
        grid=(b,),
        in_specs=[pl.BlockSpec(memory_space=pltpu.SMEM), spec(NA_W), spec(NA_W), spec(NA_W),
                  spec(WG_W), spec(WG_KV_W), spec(WG_KV_W)],
        out_specs=[spec(NA_W), spec(WG_W)],
        out_shape=[jax.ShapeDtypeStruct((b, c_len, NA_W), _BF16),
                   jax.ShapeDtypeStruct((b, c_len, WG_W), _BF16)],
        compiler_params=_cparams(1),
        name="context_attn",
    )(sink, qa, ka, va, qb, kb, vb)


HALO = 16


def _shift_rows(ext, lo, n):
    total = ext.shape[0]
    prev = pltpu.roll(ext, 1, 0)[lo:lo + n]
    nxt = pltpu.roll(ext, total - 1, 0)[lo:lo + n]
    return prev, nxt


def _tail_kernel(ya_ref, yap_ref, yan_ref, yb_ref, ybp_ref, ybn_ref, cu_ref, cup_ref, cun_ref,
                 bg_ref, bgp_ref, bgn_ref, x_ref, xp_ref, xn_ref, mod_ref, g_ref, convc_ref, wo_ref,
                 wu_ref, convf_ref, wd_ref, o_ref, h_scr, act_scr):
    i = pl.program_id(1)
    tile = x_ref.shape[0]
    ext_rows = tile + 2 * HALO
    prev_ok = (i > 0).astype(_F32)
    next_ok = (i < pl.num_programs(1) - 1).astype(_F32)
    cat = lambda p, m, n: jnp.concatenate([p, m, n], axis=0)

    cu = cat(cup_ref[...] * prev_ok, cu_ref[...], cun_ref[...] * next_ok)
    left, right = _shift_rows(cu, 0, ext_rows)
    yc = cat(bgp_ref[...], bg_ref[...], bgn_ref[...]) * (
        left * convc_ref[0:1, :] + cu * convc_ref[1:2, :] + right * convc_ref[2:3, :])
    y = (_dot(cat(yap_ref[...], ya_ref[...], yan_ref[...]), wo_ref[0:NA_W, :])
         + _dot(cat(ybp_ref[...], yb_ref[...], ybn_ref[...]), wo_ref[NA_W:NA_W + WG_W, :])
         + _dot(yc.astype(_BF16), wo_ref[NA_W + WG_W:, :]))
    xm = cat(xp_ref[...], x_ref[...], xn_ref[...]) + mod_ref[2:3, :] * y

    win = tile + 2 * SUBLANES
    xw = xm[HALO - SUBLANES:HALO - SUBLANES + win]
    row = lax.broadcasted_iota(jnp.int32, (win, 1), 0)
    inside = jnp.where(row < SUBLANES, prev_ok, jnp.where(row >= SUBLANES + tile, next_ok, 1.0))
    h_scr[...] = (_norm_mod(xw, g_ref[...], mod_ref[3:4, :], mod_ref[4:5, :]) * inside).astype(_BF16)

    for c in range(N_FF_CHUNKS):
        a_cols = slice(c * FF_CHUNK, (c + 1) * FF_CHUNK)
        g_cols = slice(D_FF + c * FF_CHUNK, D_FF + (c + 1) * FF_CHUNK)
        h = h_scr[...]
        ua = _dot(h, wu_ref[:, a_cols])
        ug = _dot(h, wu_ref[:, g_cols])
        pa, na = _shift_rows(ua, SUBLANES, tile)
        pg, ng = _shift_rows(ug, SUBLANES, tile)
        mid = slice(SUBLANES, SUBLANES + tile)
        a = (pa * convf_ref[0:1, a_cols] + ua[mid] * convf_ref[1:2, a_cols]
             + na * convf_ref[2:3, a_cols])
        gt = (pg * convf_ref[0:1, g_cols] + ug[mid] * convf_ref[1:2, g_cols]
              + ng * convf_ref[2:3, g_cols])
        act_scr[:, a_cols] = ((a / (1.0 + jnp.exp(-a))) * gt).astype(_BF16)

    o_ref[...] = xm[HALO:HALO + tile] + mod_ref[5:6, :] * _dot(act_scr[...], wd_ref[...])


def _tile_and_halo_specs(tile, width, n_rows):
    per = tile // HALO
    last = n_rows // HALO - 1
    main = pl.BlockSpec((None, tile, width), lambda bi, i: (bi, i, 0))
    prev = pl.BlockSpec((None, HALO, width), lambda bi, i: (bi, jnp.maximum(i * per - 1, 0), 0))
    nxt = pl.BlockSpec((None, HALO, width), lambda bi, i: (bi, jnp.minimum((i + 1) * per, last), 0))
    return [main, prev, nxt]


def _tail_call(ya, yb, cu, bg, x, mod, g, conv_c, w_o, wu, conv_f, wd, tile):
    b, s, _ = x.shape
    const = lambda shape: pl.BlockSpec(shape, lambda bi, i: (0,) * len(shape),
                                       pipeline_mode=pl.Buffered(1))
    in_specs = []
    for width in (NA_W, WG_W, SC_CH, SC_CH, D_MODEL):
        in_specs += _tile_and_halo_specs(tile, width, s)
    in_specs += [pl.BlockSpec((None, 6, D_MODEL), lambda bi, i: (bi, 0, 0)),
                 const((1, D_MODEL)), const(conv_c.shape), const(w_o.shape),
                 const(wu.shape), const(conv_f.shape), const(wd.shape)]
    return pl.pallas_call(
        _tail_kernel,
        grid=(b, s // tile),
        in_specs=in_specs,
        out_specs=pl.BlockSpec((None, tile, D_MODEL), lambda bi, i: (bi, i, 0)),
        out_shape=jax.ShapeDtypeStruct((b, s, D_MODEL), _F32),
        scratch_shapes=[pltpu.VMEM((tile + 2 * SUBLANES, D_MODEL), _BF16),
                        pltpu.VMEM((tile, D_FF), _BF16)],
        compiler_params=_cparams(2),
        name="layer_tail",
    )(ya, ya, ya, yb, yb, yb, cu, cu, cu, bg, bg, bg, x, x, x, mod, g, conv_c, w_o, wu, conv_f, wd)


def _wg_head_perm():
    order = []
    for c in range(WG_HEADS // 2):
        order += [c, c + WG_HEADS // 2]
    return np.concatenate([np.arange(h * HEAD_DIM, (h + 1) * HEAD_DIM) for h in order])


def _rope_tables(seq):
    t = np.arange(seq)
    row = (t // GRID_W).astype(np.float64)
    col = (t % GRID_W).astype(np.float64)
    half = HEAD_DIM // 2
    n_freq = half // 2
    inv = ROPE_BASE ** (-np.arange(n_freq, dtype=np.float64) / n_freq)
    ang = np.concatenate([row[:, None] * inv, col[:, None] * inv], axis=-1)
    cos = np.concatenate([np.cos(ang), np.cos(ang)], axis=-1)
    sin = np.concatenate([-np.sin(ang), np.sin(ang)], axis=-1)
    reps = LANES // HEAD_DIM
    return (jnp.asarray(np.tile(cos, (1, reps)), _F32), jnp.asarray(np.tile(sin, (1, reps)), _F32))


def _na_bias_table(rpb, n_rows):
    wr = min(NA_WIN_R, n_rows)
    q = np.arange(GRID_W)
    c0 = np.clip(q - NA_WIN_C // 2, 0, GRID_W - NA_WIN_C)
    kc = np.arange(GRID_W)
    ok = (kc[None, :] >= c0[:, None]) & (kc[None, :] < c0[:, None] + NA_WIN_C)
    pad = GRID_W - NA_WIN_C
    padded = jnp.pad(rpb.astype(_F32), ((0, 0), (0, 0), (pad, pad)))
    toep = jnp.stack([padded[:, :, GRID_W - 1 - qi:2 * GRID_W - 1 - qi] for qi in range(GRID_W)],
                     axis=1)
    toep = jnp.where(ok[None, :, None, :], toep * LOG2E, NEG)
    n_d = 2 * NA_WIN_R - 1
    toep = toep.reshape(NA_HEADS, GRID_W, n_d * GRID_W)
    return jnp.stack([toep[:, :, off * GRID_W:(off + wr) * GRID_W] for off in range(NA_WIN_R)],
                     axis=0)


def _block_diag_mean():
    idx = np.arange(MXU_DIM) // HEAD_DIM
    return jnp.asarray((idx[:, None] == idx[None, :]) / HEAD_DIM, _BF16)


def _pad_rows(a, rows):
    return jnp.concatenate([a, jnp.zeros((rows - a.shape[0],) + a.shape[1:], a.dtype)], axis=0)


def kernel(x, c, ctx, c_ctx, w_ada, b_ada, g_attn, w_in, qn_a, kn_a, qn_b, kn_b, rpb_a, sink_b,
           conv_c, w_o, g_ffn, w_up, conv_ffn, w_down):
    batch, seq, _ = x.shape
    c_len = ctx.shape[1]
    n_rows = seq // GRID_W
    scale = LOG2E / math.sqrt(HEAD_DIM)
    perm = _wg_head_perm()

    cvec = _pad_rows(jnp.concatenate([c, c_ctx[None, :]], axis=0), 2 * SUBLANES)
    mod = _ada_call(cvec, w_ada, b_ada)

    cos, sin = _rope_tables(seq)
    ones_c = jnp.ones((c_len, LANES), _F32)
    zeros_c = jnp.zeros((c_len, LANES), _F32)
    e = _block_diag_mean()

    xl, xc = x, ctx
    for l in range(DEPTH):
        update_ctx = l < DEPTH - 1
        mod_l = mod[l, :batch].reshape(batch, 6, D_MODEL)
        mod_c = jnp.broadcast_to(mod[l, batch].reshape(1, 6, D_MODEL), (batch, 6, D_MODEL))

        w_in_l = w_in[l]
        w_in_l = jnp.concatenate(
            [w_in_l[:, :_QB], w_in_l[:, _QB:_KB][:, perm], w_in_l[:, _KB:]], axis=1).astype(_BF16)
        w_o_l = w_o[l]
        w_o_l = jnp.concatenate(
            [w_o_l[:NA_W], w_o_l[NA_W:NA_W + WG_W][perm], w_o_l[NA_W + WG_W:]], axis=0).astype(_BF16)
        tile4 = lambda v: jnp.tile(v, MXU_DIM // HEAD_DIM)
        gains = _pad_rows(jnp.stack([tile4(qn_a[l]) * scale, tile4(kn_a[l]),
                                     tile4(qn_b[l]) * scale, tile4(kn_b[l])]), SUBLANES)
        bias = _na_bias_table(rpb_a[l], n_rows)
        conv_c_l = _pad_rows(conv_c[l], SUBLANES)
        g_attn_l = g_attn[l].reshape(1, D_MODEL)
        g_ffn_l = g_ffn[l].reshape(1, D_MODEL)
        wu = w_up[l].astype(_BF16)
        wd = w_down[l].astype(_BF16)
        taps = _pad_rows(conv_ffn[l], SUBLANES)
        sink_perm = sink_b[l] * LOG2E

        qa, ka, va, qb, kb, vb, cu, bg = _inproj_call(
            xl, mod_l, g_attn_l, w_in_l, gains, e, cos, sin, TOKEN_TILE)
        qa_c, ka_c, va_c, qb_c, kb_c, vb_c, cu_c, bg_c = _inproj_call(
            xc, mod_c, g_attn_l, w_in_l, gains, e, ones_c, zeros_c, c_len)
        ya = _na_call(qa, ka, va, ka_c, va_c, bias)
        yb = _wg_call(sink_perm, qb, kb, vb, kb_c, vb_c)
        xl = _tail_call(ya, yb, cu, bg, xl, mod_l, g_ffn_l, conv_c_l, w_o_l, wu, taps, wd,
                        TOKEN_TILE)
        if update_ctx:
            ya_c, yb_c = _ctx_attn_call(sink_perm, qa_c, ka_c, va_c, qb_c, kb_c, vb_c)
            xc = _tail_call(ya_c, yb_c, cu_c, bg_c, xc, mod_c, g_ffn_l, conv_c_l, w_o_l, wu, taps,
                            wd, c_len)
    return xl
```

```python
import functools
import math

import jax
import jax.numpy as jnp
import numpy as np
from jax import lax
from jax.experimental import pallas as pl
from jax.experimental.pallas import tpu as pltpu

D_MODEL = 1024
DEPTH = 2
GRID_W = 64
HEAD_DIM = 64
NA_HEADS = 6
NA_WIN_R = 8
NA_WIN_C = 16
WG_HEADS = 6
WG_KV_HEADS = 2
WG_WINDOW = 128
WG_BLOCK = 128
SC_CH = 256
CONV_W = 3
D_FF = 2816
ROPE_BASE = 10000.0
EPS = 1e-6
NEG = -1e30
LOG2E = math.log2(math.e)

NA_W = NA_HEADS * HEAD_DIM
WG_W = WG_HEADS * HEAD_DIM
WG_KV_W = WG_KV_HEADS * HEAD_DIM
IN_W = 3 * NA_W + WG_W + 2 * WG_KV_W + 3 * SC_CH

LANES = 128
SUBLANES = 8
MXU_DIM = 256
FF_CHUNK = MXU_DIM
N_FF_CHUNKS = D_FF // FF_CHUNK
TOKEN_TILE = 512
INPROJ_SUB_TILE = 256
NA_ROWS_PER_STEP = 8
SOFTMAX_ROWS = 32
WG_BLOCKS_PER_STEP = 4
VMEM_LIMIT = 48 * 1024 * 1024

_QA, _KA, _VA = 0, NA_W, 2 * NA_W
_QB = 3 * NA_W
_KB = _QB + WG_W
_VB = _KB + WG_KV_W
_U = _VB + WG_KV_W
_BG = _U + SC_CH
_CG = _BG + SC_CH

_F32 = jnp.float32
_BF16 = jnp.bfloat16


def _dot(a, b):
    return jnp.dot(a, b, preferred_element_type=_F32)


def _dot_nt(a, b):
    return lax.dot_general(a, b, (((1,), (1,)), ((), ())), preferred_element_type=_F32)


def _cparams(n_grid):
    return pltpu.CompilerParams(
        dimension_semantics=("arbitrary",) * n_grid, vmem_limit_bytes=VMEM_LIMIT)


def _ada_kernel(c_ref, w_ref, b_ref, o_ref):
    c = c_ref[...]
    s = c / (1.0 + jnp.exp(-c))
    s_hi = s.astype(_BF16)
    s_lo = (s - s_hi.astype(_F32)).astype(_BF16)
    w = w_ref[...]
    w_hi = w.astype(_BF16)
    w_lo = (w - w_hi.astype(_F32)).astype(_BF16)
    o_ref[...] = _dot(s_hi, w_hi) + _dot(s_lo, w_hi) + _dot(s_hi, w_lo) + b_ref[...]


def _ada_call(cvec, w_ada, b_ada):
    rows = cvec.shape[0]
    n_out = w_ada.shape[-1]
    nt = 1536
    return pl.pallas_call(
        _ada_kernel,
        grid=(DEPTH, n_out // nt),
        in_specs=[
            pl.BlockSpec((rows, D_MODEL), lambda l, j: (0, 0)),
            pl.BlockSpec((None, D_MODEL, nt), lambda l, j: (l, 0, j)),
            pl.BlockSpec((None, 1, nt), lambda l, j: (l, 0, j)),
        ],
        out_specs=pl.BlockSpec((None, rows, nt), lambda l, j: (l, 0, j)),
        out_shape=jax.ShapeDtypeStruct((DEPTH, rows, n_out), _F32),
        compiler_params=_cparams(2),
        name="adaln_mod",
    )(cvec, w_ada, b_ada.reshape(DEPTH, 1, n_out))


def _norm_mod(x, gain, shift):
    ms = jnp.mean(x * x, axis=-1, keepdims=True)
    return x * lax.rsqrt(ms + EPS) * gain + shift


def _lane_id(shape):
    return lax.broadcasted_iota(jnp.int32, shape, len(shape) - 1)


def _head_norm(x, gain):
    outs = []
    for j in range(x.shape[1] // LANES):
        xc = x[:, j * LANES:(j + 1) * LANES]
        sq = xc * xc
        low = _lane_id(xc.shape) < HEAD_DIM
        s_lo = jnp.sum(jnp.where(low, sq, 0.0), axis=-1, keepdims=True)
        s_hi = jnp.sum(jnp.where(low, 0.0, sq), axis=-1, keepdims=True)
        ms = jnp.where(low, s_lo, s_hi) * (1.0 / HEAD_DIM)
        outs.append(xc * lax.rsqrt(ms + EPS))
    return jnp.concatenate(outs, axis=1) * gain


def _inproj_kernel(x_ref, mod_ref, g_ref, w_ref, gain_ref, cos_ref, sin_ref,
                   qa_ref, ka_ref, va_ref, qb_ref, kb_ref, vb_ref, cu_ref, bg_ref, p_scr, *, sub):
    gain_qk_a = jnp.concatenate([gain_ref[0:1, :LANES], gain_ref[1:2, :LANES]], axis=1)
    gain_qk_b = jnp.concatenate([gain_ref[2:3, :LANES], gain_ref[3:4, :LANES]], axis=1)
    first_half = (_lane_id((sub, LANES)) % HEAD_DIM) < (HEAD_DIM // 2)
    norm_gain = g_ref[...] * (1.0 + mod_ref[1:2, :])
    for t in range(x_ref.shape[0] // sub):
        rows = slice(t * sub, (t + 1) * sub)
        h = _norm_mod(x_ref[rows, :], norm_gain, mod_ref[0:1, :]).astype(_BF16)
        p_scr[rows, :] = _dot(h, w_ref[...])

        p = p_scr[rows, _QA:_QA + MXU_DIM]
        qa_ref[rows, 0:MXU_DIM] = _head_norm(p,gain_ref[0:1, :]).astype(_BF16)
        p = p_scr[rows, _QA + MXU_DIM:_QA + 2 * MXU_DIM]
        pn = _head_norm(p,gain_qk_a).astype(_BF16)
        qa_ref[rows, MXU_DIM:NA_W] = pn[:, :LANES]
        ka_ref[rows, 0:LANES] = pn[:, LANES:]
        p = p_scr[rows, _QA + 2 * MXU_DIM:_QA + 3 * MXU_DIM]
        ka_ref[rows, LANES:NA_W] = _head_norm(p,gain_ref[1:2, :]).astype(_BF16)

        va_ref[rows, :] = p_scr[rows, _VA:_VA + NA_W].astype(_BF16)

        cos = cos_ref[rows, :]
        sin = sin_ref[rows, :]
        for k in range(2):
            p = p_scr[rows, _QB + k * MXU_DIM:_QB + (k + 1) * MXU_DIM]
            pn = _head_norm(p,gain_ref[2:3, :] if k == 0 else gain_qk_b)
            for j in range(2):
                xc = pn[:, j * LANES:(j + 1) * LANES]
                swapped = jnp.where(first_half, pltpu.roll(xc, LANES - HEAD_DIM // 2, 1),
                                    pltpu.roll(xc, HEAD_DIM // 2, 1))
                xr = (xc * cos + swapped * sin).astype(_BF16)
                col = k * MXU_DIM + j * LANES
                if col < WG_W:
                    qb_ref[rows, col:col + LANES] = xr
                else:
                    kb_ref[rows, :] = xr

        vb_ref[rows, :] = p_scr[rows, _VB:_VB + WG_KV_W].astype(_BF16)
        bg_ref[rows, :] = p_scr[rows, _BG:_BG + SC_CH]
        cu_ref[rows, :] = p_scr[rows, _CG:_CG + SC_CH] * p_scr[rows, _U:_U + SC_CH]


def _inproj_call(x, mod, g, w_in, gains, cos, sin, tile):
    b, s, _ = x.shape
    nt = s // tile
    tok = lambda width: pl.BlockSpec((None, tile, width), lambda bi, i: (bi, i, 0))
    const = lambda shape: pl.BlockSpec(shape, lambda bi, i: (0,) * len(shape))
    widths = (NA_W, NA_W, NA_W, WG_W, WG_KV_W, WG_KV_W, SC_CH, SC_CH)
    dtypes = (_BF16,) * 6 + (_F32, _F32)
    return pl.pallas_call(
        functools.partial(_inproj_kernel, sub=min(tile, INPROJ_SUB_TILE)),
        grid=(b, nt),
        in_specs=[
            tok(D_MODEL),
            pl.BlockSpec((None, 6, D_MODEL), lambda bi, i: (bi, 0, 0)),
            const((1, D_MODEL)),
            const((D_MODEL, IN_W)),
            const((SUBLANES, MXU_DIM)),
            pl.BlockSpec((tile, LANES), lambda bi, i: (i, 0)),
            pl.BlockSpec((tile, LANES), lambda bi, i: (i, 0)),
        ],
        out_specs=[tok(w) for w in widths],
        out_shape=[jax.ShapeDtypeStruct((b, s, w), d) for w, d in zip(widths, dtypes)],
        scratch_shapes=[pltpu.VMEM((tile, IN_W), _F32)],
        compiler_params=_cparams(2),
        name="mixer_inproj",
    )(x, mod, g, w_in, gains, cos, sin)


def _masked_halves(qp):
    qf = qp.astype(_F32)
    low = _lane_id(qf.shape) < HEAD_DIM
    return (jnp.where(low, qf, 0.0).astype(_BF16), jnp.where(low, 0.0, qf).astype(_BF16))


def _softmax_parts(parts, extra=None):
    m = parts[0].max(axis=-1, keepdims=True)
    for s in parts[1:]:
        m = jnp.maximum(m, s.max(axis=-1, keepdims=True))
    if extra is not None:
        m = jnp.maximum(m, extra)
    ps = [jnp.exp2(s - m) for s in parts]
    denom = ps[0].sum(axis=-1, keepdims=True)
    for p in ps[1:]:
        denom = denom + p.sum(axis=-1, keepdims=True)
    if extra is not None:
        denom = denom + jnp.exp2(extra - m)
    return [p.astype(_BF16) for p in ps], denom


def _na_kernel(q_ref, k_ref, v_ref, kc_ref, vc_ref, bias_ref, o_ref, *, rows_per_step, n_rows):
    step = pl.program_id(1)
    wr = min(NA_WIN_R, n_rows)
    low = _lane_id((GRID_W, LANES)) < HEAD_DIM
    offs, kstarts = [], []
    for rr in range(rows_per_step):
        r = step * rows_per_step + rr
        r0 = jnp.clip(r - wr // 2, 0, n_rows - wr)
        offs.append(r0 - r + NA_WIN_R - 1)
        kstarts.append(pl.multiple_of(r0 * GRID_W, GRID_W))

    for c in range(NA_HEADS // 2):
        cols = slice(c * LANES, (c + 1) * LANES)
        lhs = jnp.concatenate(
            [half for rr in range(rows_per_step)
             for half in _masked_halves(q_ref[rr * GRID_W:(rr + 1) * GRID_W, cols])], axis=0)
        s_ctx = _dot_nt(lhs, kc_ref[:, cols])
        p_ctx, o_loc, denoms = [], [], []
        for rr in range(rows_per_step):
            rows = slice(rr * 2 * GRID_W, (rr + 1) * 2 * GRID_W)
            kl = k_ref[pl.ds(kstarts[rr], wr * GRID_W), cols]
            vl = v_ref[pl.ds(kstarts[rr], wr * GRID_W), cols]
            bias = jnp.concatenate(
                [bias_ref[offs[rr], 2 * c], bias_ref[offs[rr], 2 * c + 1]], axis=0)
            s_loc = _dot_nt(lhs[rows], kl) + bias
            (pl_, pc_), dn = _softmax_parts([s_loc, s_ctx[rows]])
            o_loc.append(_dot(pl_, vl))
            p_ctx.append(pc_)
            denoms.append(dn)
        o_ctx = _dot(jnp.concatenate(p_ctx, axis=0), vc_ref[:, cols])
        for rr in range(rows_per_step):
            rows = slice(rr * 2 * GRID_W, (rr + 1) * 2 * GRID_W)
            o = (o_loc[rr] + o_ctx[rows]) / denoms[rr]
            o_ref[rr * GRID_W:(rr + 1) * GRID_W, cols] = jnp.where(
                low, o[:GRID_W], o[GRID_W:]).astype(_BF16)


def _na_call(qa, ka, va, kc, vc, bias):
    b, s, _ = qa.shape
    n_rows = s // GRID_W
    c_len = kc.shape[1]
    rps = NA_ROWS_PER_STEP
    blk = rps * GRID_W
    return pl.pallas_call(
        functools.partial(_na_kernel, rows_per_step=rps, n_rows=n_rows),
        grid=(b, n_rows // rps),
        in_specs=[
            pl.BlockSpec((None, blk, NA_W), lambda bi, j: (bi, j, 0)),
            pl.BlockSpec((None, s, NA_W), lambda bi, j: (bi, 0, 0)),
            pl.BlockSpec((None, s, NA_W), lambda bi, j: (bi, 0, 0)),
            pl.BlockSpec((None, c_len, NA_W), lambda bi, j: (bi, 0, 0)),
            pl.BlockSpec((None, c_len, NA_W), lambda bi, j: (bi, 0, 0)),
            pl.BlockSpec(bias.shape, lambda bi, j: (0, 0, 0, 0)),
        ],
        out_specs=pl.BlockSpec((None, blk, NA_W), lambda bi, j: (bi, j, 0)),
        out_shape=jax.ShapeDtypeStruct((b, s, NA_W), _BF16),
        compiler_params=_cparams(2),
        name="neighbourhood_attn",
    )(qa, ka, va, kc, vc, bias)


def _wg_window_start(blk, seq):
    span = 3 * WG_BLOCK
    return pl.multiple_of(jnp.clip((blk - 1) * WG_BLOCK, 0, seq - span), WG_BLOCK)


def _wg_scores(q_ref, k_ref, kc_ref, step, s_loc_scr, s_ctx_scr, *, seq, n_blk):
    span = 3 * WG_BLOCK
    per_blk = WG_HEADS * WG_BLOCK
    pieces = []
    for bb in range(n_blk):
        for c in range(WG_HEADS // 2):
            pieces.extend(_masked_halves(
                q_ref[bb * WG_BLOCK:(bb + 1) * WG_BLOCK, c * LANES:(c + 1) * LANES]))
    lhs = jnp.concatenate(pieces, axis=0)
    s_ctx_scr[...] = _dot_nt(lhs, kc_ref[...])
    for bb in range(n_blk):
        start = _wg_window_start(step * n_blk + bb, seq)
        s_loc_scr[bb * per_blk:(bb + 1) * per_blk, :] = _dot_nt(
            lhs[bb * per_blk:(bb + 1) * per_blk], k_ref[pl.ds(start, span), :])


def _wg_softmax_pv(sink_ref, s_loc_scr, s_ctx_scr, v_ref, vc_ref, step, o_ref,
                   p_loc_scr, p_ctx_scr, dn_scr, *, seq, n_blk):
    span = 3 * WG_BLOCK
    per_blk = WG_HEADS * WG_BLOCK
    low = _lane_id((WG_BLOCK, LANES)) < HEAD_DIM
    row_id = lax.broadcasted_iota(jnp.int32, (SOFTMAX_ROWS, span), 0)
    col_id = lax.broadcasted_iota(jnp.int32, (SOFTMAX_ROWS, span), 1)
    o_loc = []
    for bb in range(n_blk):
        blk = step * n_blk + bb
        start = _wg_window_start(blk, seq)
        for rc in range(WG_BLOCK // SOFTMAX_ROWS):
            dist = (blk * WG_BLOCK - start + rc * SOFTMAX_ROWS) + row_id - col_id
            mask_add = jnp.where(jnp.abs(dist) <= WG_WINDOW, 0.0, NEG).astype(_F32)
            for hb in range(WG_HEADS):
                head = (hb // 2) + (WG_HEADS // 2) * (hb % 2)
                r0 = bb * per_blk + hb * WG_BLOCK + rc * SOFTMAX_ROWS
                rows = slice(r0, r0 + SOFTMAX_ROWS)
                (pl_, pc_), dn = _softmax_parts(
                    [s_loc_scr[rows, :] + mask_add, s_ctx_scr[rows, :]], sink_ref[head])
                p_loc_scr[rows, :] = pl_
                p_ctx_scr[rows, :] = pc_
                dn_scr[rows, :] = jnp.broadcast_to(dn, (SOFTMAX_ROWS, LANES))
        o_loc.append(_dot(p_loc_scr[bb * per_blk:(bb + 1) * per_blk, :],
                          v_ref[pl.ds(start, span), :]))
    o_ctx = _dot(p_ctx_scr[...], vc_ref[...])
    for bb in range(n_blk):
        for c in range(WG_HEADS // 2):
            halves = []
            for half in range(2):
                hb = 2 * c + half
                rows = slice(hb * WG_BLOCK, (hb + 1) * WG_BLOCK)
                rows_all = slice(bb * per_blk + hb * WG_BLOCK, bb * per_blk + (hb + 1) * WG_BLOCK)
                halves.append((o_loc[bb][rows] + o_ctx[rows_all]) / dn_scr[rows_all, :])
            o_ref[bb * WG_BLOCK:(bb + 1) * WG_BLOCK, c * LANES:(c + 1) * LANES] = jnp.where(
                low, halves[0], halves[1]).astype(_BF16)


def _wg_kernel(sink_ref, q_ref, qn_ref, k_ref, kn_ref, v_ref, kc_ref, kcn_ref, vc_ref, o_ref,
               sa_loc, sa_ctx, sb_loc, sb_ctx, p_loc_scr, p_ctx_scr, dn_scr, *, seq, n_blk):
    bi, step = pl.program_id(0), pl.program_id(1)
    nxt = (step + 1) % pl.num_programs(1)
    scores = functools.partial(_wg_scores, seq=seq, n_blk=n_blk)
    finish = functools.partial(_wg_softmax_pv, sink_ref, seq=seq, n_blk=n_blk)

    @pl.when((bi == 0) & (step == 0))
    def _():
        scores(q_ref, k_ref, kc_ref, step, sa_loc, sa_ctx)

    @pl.when(step % 2 == 0)
    def _():
        scores(qn_ref, kn_ref, kcn_ref, nxt, sb_loc, sb_ctx)
        finish(sa_loc, sa_ctx, v_ref, vc_ref, step, o_ref, p_loc_scr, p_ctx_scr, dn_scr)

    @pl.when(step % 2 == 1)
    def _():
        scores(qn_ref, kn_ref, kcn_ref, nxt, sa_loc, sa_ctx)
        finish(sb_loc, sb_ctx, v_ref, vc_ref, step, o_ref, p_loc_scr, p_ctx_scr, dn_scr)


def _next_step_maps(n_batch, n_steps):
    nb = lambda bi, i: jnp.minimum(bi + (i + 1) // n_steps, n_batch - 1)
    return (lambda bi, i: (nb(bi, i), (i + 1) % n_steps, 0)), (lambda bi, i: (nb(bi, i), 0, 0))


def _wg_call(sink, qb, kb, vb, kc, vc):
    b, s, _ = qb.shape
    c_len = kc.shape[1]
    n_blk = WG_BLOCKS_PER_STEP
    q_rows = n_blk * WG_BLOCK
    n_steps = s // q_rows
    assert n_steps % 2 == 0
    next_q, next_kv = _next_step_maps(b, n_steps)
    cur_q = lambda bi, i: (bi, i, 0)
    cur_kv = lambda bi, i: (bi, 0, 0)
    s_rows = n_blk * WG_HEADS * WG_BLOCK
    return pl.pallas_call(
        functools.partial(_wg_kernel, seq=s, n_blk=n_blk),
        grid=(b, n_steps),
        in_specs=[
            pl.BlockSpec(memory_space=pltpu.SMEM),
            pl.BlockSpec((None, q_rows, WG_W), cur_q),
            pl.BlockSpec((None, q_rows, WG_W), next_q),
            pl.BlockSpec((None, s, WG_KV_W), cur_kv),
            pl.BlockSpec((None, s, WG_KV_W), next_kv),
            pl.BlockSpec((None, s, WG_KV_W), cur_kv),
            pl.BlockSpec((None, c_len, WG_KV_W), cur_kv),
            pl.BlockSpec((None, c_len, WG_KV_W), next_kv),
            pl.BlockSpec((None, c_len, WG_KV_W), cur_kv),
        ],
        out_specs=pl.BlockSpec((None, q_rows, WG_W), cur_q),
        out_shape=jax.ShapeDtypeStruct((b, s, WG_W), _BF16),
        scratch_shapes=[pltpu.VMEM((s_rows, 3 * WG_BLOCK), _F32), pltpu.VMEM((s_rows, c_len), _F32),
                        pltpu.VMEM((s_rows, 3 * WG_BLOCK), _F32), pltpu.VMEM((s_rows, c_len), _F32),
                        pltpu.VMEM((s_rows, 3 * WG_BLOCK), _BF16), pltpu.VMEM((s_rows, c_len), _BF16),
                        pltpu.VMEM((s_rows, LANES), _F32)],
        compiler_params=_cparams(2),
        name="window_gqa",
    )(sink, qb, qb, kb, kb, vb, kc, kc, vc)


def _ctx_attn_kernel(sink_ref, qa_ref, ka_ref, va_ref, qb_ref, kb_ref, vb_ref, ya_ref, yb_ref):
    c_len = qa_ref.shape[0]
    low = _lane_id((c_len, LANES)) < HEAD_DIM
    for c in range(NA_HEADS // 2):
        cols = slice(c * LANES, (c + 1) * LANES)
        lhs = jnp.concatenate(_masked_halves(qa_ref[:, cols]), axis=0)
        s = _dot_nt(lhs, ka_ref[:, cols])
        (p,), denom = _softmax_parts([s])
        o = _dot(p, va_ref[:, cols]) / denom
        ya_ref[:, cols] = jnp.where(low, o[:c_len], o[c_len:]).astype(_BF16)
    kb = kb_ref[...]
    vb = vb_ref[...]
    for c in range(WG_HEADS // 2):
        cols = slice(c * LANES, (c + 1) * LANES)
        halves = _masked_halves(qb_ref[:, cols])
        outs = []
        for half in range(2):
            head = c + (WG_HEADS // 2) * half
            s = _dot_nt(halves[half], kb)
            (p,), denom = _softmax_parts([s], sink_ref[head])
            outs.append(_dot(p, vb) / denom)
        yb_ref[:, cols] = jnp.where(low, outs[0], outs[1]).astype(_BF16)


def _ctx_attn_call(sink, qa, ka, va, qb, kb, vb):
    b, c_len, _ = qa.shape
    spec = lambda w: pl.BlockSpec((None, c_len, w), lambda bi: (bi, 0, 0))
    return pl.pallas_call(
        _ctx_attn_kernel,
        grid=(b,),
        in_specs=[pl.BlockSpec(memory_space=pltpu.SMEM), spec(NA_W), spec(NA_W), spec(NA_W),
                  spec(WG_W), spec(WG_KV_W), spec(WG_KV_W)],
        out_specs=[spec(NA_W), spec(WG_W)],
        out_shape=[jax.ShapeDtypeStruct((b, c_len, NA_W), _BF16),
                   jax.ShapeDtypeStruct((b, c_len, WG_W), _BF16)],
        compiler_params=_cparams(1),
        name="context_attn",
    )(sink, qa, ka, va, qb, kb, vb)


HALO = 16


def _shift_rows(ext, lo, n):
    total = ext.shape[0]
    prev = pltpu.roll(ext, 1, 0)[lo:lo + n]
    nxt = pltpu.roll(ext, total - 1, 0)[lo:lo + n]
    return prev, nxt


def _tail_kernel(ya_ref, yap_ref, yan_ref, yb_ref, ybp_ref, ybn_ref, cu_ref, cup_ref, cun_ref,
                 bg_ref, bgp_ref, bgn_ref, x_ref, xp_ref, xn_ref, mod_ref, g_ref, convc_ref, wo_ref,
                 wu_ref, convf_ref, wd_ref, o_ref, xm_scr, h_scr, act_scr):
    i = pl.program_id(1)
    tile = x_ref.shape[0]
    ext_rows = tile + 2 * HALO
    prev_ok = (i > 0).astype(_F32)
    next_ok = (i < pl.num_programs(1) - 1).astype(_F32)
    cat = lambda p, m, n: jnp.concatenate([p, m, n], axis=0)

    cu = cat(cup_ref[...] * prev_ok, cu_ref[...], cun_ref[...] * next_ok)
    left, right = _shift_rows(cu, 0, ext_rows)
    yc = cat(bgp_ref[...], bg_ref[...], bgn_ref[...]) * (
        left * convc_ref[0:1, :] + cu * convc_ref[1:2, :] + right * convc_ref[2:3, :])
    ya = cat(yap_ref[...], ya_ref[...], yan_ref[...])
    yb = cat(ybp_ref[...], yb_ref[...], ybn_ref[...])
    yc = yc.astype(_BF16)
    x_ext = cat(xp_ref[...], x_ref[...], xn_ref[...])
    norm_gain = g_ref[...] * (1.0 + mod_ref[4:5, :])
    half = ext_rows // 2
    for hh in range(2):
        rows = slice(hh * half, (hh + 1) * half)
        y = (_dot(ya[rows], wo_ref[0:NA_W, :]) + _dot(yb[rows], wo_ref[NA_W:NA_W + WG_W, :])
             + _dot(yc[rows], wo_ref[NA_W + WG_W:, :]))
        xm = x_ext[rows] + mod_ref[2:3, :] * y
        xm_scr[rows, :] = xm
        row = hh * half + lax.broadcasted_iota(jnp.int32, (half, 1), 0)
        inside = jnp.where(row < HALO, prev_ok, jnp.where(row >= HALO + tile, next_ok, 1.0))
        h_scr[rows, :] = (_norm_mod(xm, norm_gain, mod_ref[3:4, :]) * inside).astype(_BF16)

    for c in range(N_FF_CHUNKS):
        a_cols = slice(c * FF_CHUNK, (c + 1) * FF_CHUNK)
        g_cols = slice(D_FF + c * FF_CHUNK, D_FF + (c + 1) * FF_CHUNK)
        h = h_scr[...]
        ua = _dot(h, wu_ref[:, a_cols])
        ug = _dot(h, wu_ref[:, g_cols])
        pa, na = _shift_rows(ua, HALO, tile)
        pg, ng = _shift_rows(ug, HALO, tile)
        mid = slice(HALO, HALO + tile)
        a = (pa * convf_ref[0:1, a_cols] + ua[mid] * convf_ref[1:2, a_cols]
             + na * convf_ref[2:3, a_cols])
        gt = (pg * convf_ref[0:1, g_cols] + ug[mid] * convf_ref[1:2, g_cols]
              + ng * convf_ref[2:3, g_cols])
        act_scr[:, a_cols] = ((a / (1.0 + jnp.exp(-a))) * gt).astype(_BF16)

    o_ref[...] = xm_scr[HALO:HALO + tile, :] + mod_ref[5:6, :] * _dot(act_scr[...], wd_ref[...])


def _tile_and_halo_specs(tile, width, n_rows):
    per = tile // HALO
    last = n_rows // HALO - 1
    main = pl.BlockSpec((None, tile, width), lambda bi, i: (bi, i, 0))
    prev = pl.BlockSpec((None, HALO, width), lambda bi, i: (bi, jnp.maximum(i * per - 1, 0), 0))
    nxt = pl.BlockSpec((None, HALO, width), lambda bi, i: (bi, jnp.minimum((i + 1) * per, last), 0))
    return [main, prev, nxt]


def _tail_call(ya, yb, cu, bg, x, mod, g, conv_c, w_o, wu, conv_f, wd, tile):
    b, s, _ = x.shape
    const = lambda shape: pl.BlockSpec(shape, lambda bi, i: (0,) * len(shape),
                                       pipeline_mode=pl.Buffered(1))
    in_specs = []
    for width in (NA_W, WG_W, SC_CH, SC_CH, D_MODEL):
        in_specs += _tile_and_halo_specs(tile, width, s)
    in_specs += [pl.BlockSpec((None, 6, D_MODEL), lambda bi, i: (bi, 0, 0)),
                 const((1, D_MODEL)), const(conv_c.shape), const(w_o.shape),
                 const(wu.shape), const(conv_f.shape), const(wd.shape)]
    return pl.pallas_call(
        _tail_kernel,
        grid=(b, s // tile),
        in_specs=in_specs,
        out_specs=pl.BlockSpec((None, tile, D_MODEL), lambda bi, i: (bi, i, 0)),
        out_shape=jax.ShapeDtypeStruct((b, s, D_MODEL), _F32),
        scratch_shapes=[pltpu.VMEM((tile + 2 * HALO, D_MODEL), _F32),
                        pltpu.VMEM((tile + 2 * HALO, D_MODEL), _BF16),
                        pltpu.VMEM((tile, D_FF), _BF16)],
        compiler_params=_cparams(2),
        name="layer_tail",
    )(ya, ya, ya, yb, yb, yb, cu, cu, cu, bg, bg, bg, x, x, x, mod, g, conv_c, w_o, wu, conv_f, wd)


def _wg_head_perm():
    order = []
    for c in range(WG_HEADS // 2):
        order += [c, c + WG_HEADS // 2]
    return np.concatenate([np.arange(h * HEAD_DIM, (h + 1) * HEAD_DIM) for h in order])


def _rope_tables(seq):
    t = np.arange(seq)
    row = (t // GRID_W).astype(np.float64)
    col = (t % GRID_W).astype(np.float64)
    half = HEAD_DIM // 2
    n_freq = half // 2
    inv = ROPE_BASE ** (-np.arange(n_freq, dtype=np.float64) / n_freq)
    ang = np.concatenate([row[:, None] * inv, col[:, None] * inv], axis=-1)
    cos = np.concatenate([np.cos(ang), np.cos(ang)], axis=-1)
    sin = np.concatenate([-np.sin(ang), np.sin(ang)], axis=-1)
    reps = LANES // HEAD_DIM
    return (jnp.asarray(np.tile(cos, (1, reps)), _F32), jnp.asarray(np.tile(sin, (1, reps)), _F32))


def _na_bias_table(rpb, n_rows):
    wr = min(NA_WIN_R, n_rows)
    q = np.arange(GRID_W)
    c0 = np.clip(q - NA_WIN_C // 2, 0, GRID_W - NA_WIN_C)
    kc = np.arange(GRID_W)
    ok = (kc[None, :] >= c0[:, None]) & (kc[None, :] < c0[:, None] + NA_WIN_C)
    pad = GRID_W - NA_WIN_C
    padded = jnp.pad(rpb.astype(_F32), ((0, 0), (0, 0), (pad, pad)))
    toep = jnp.stack([padded[:, :, GRID_W - 1 - qi:2 * GRID_W - 1 - qi] for qi in range(GRID_W)],
                     axis=1)
    toep = jnp.where(ok[None, :, None, :], toep * LOG2E, NEG)
    n_d = 2 * NA_WIN_R - 1
    toep = toep.reshape(NA_HEADS, GRID_W, n_d * GRID_W)
    return jnp.stack([toep[:, :, off * GRID_W:(off + wr) * GRID_W] for off in range(NA_WIN_R)],
                     axis=0)


def _pad_rows(a, rows):
    return jnp.concatenate([a, jnp.zeros((rows - a.shape[0],) + a.shape[1:], a.dtype)], axis=0)


def kernel(x, c, ctx, c_ctx, w_ada, b_ada, g_attn, w_in, qn_a, kn_a, qn_b, kn_b, rpb_a, sink_b,
           conv_c, w_o, g_ffn, w_up, conv_ffn, w_down):
    batch, seq, _ = x.shape
    c_len = ctx.shape[1]
    n_rows = seq // GRID_W
    scale = LOG2E / math.sqrt(HEAD_DIM)
    perm = _wg_head_perm()

    cvec = _pad_rows(jnp.concatenate([c, c_ctx[None, :]], axis=0), 2 * SUBLANES)
    mod = _ada_call(cvec, w_ada, b_ada)

    cos, sin = _rope_tables(seq)
    ones_c = jnp.ones((c_len, LANES), _F32)
    zeros_c = jnp.zeros((c_len, LANES), _F32)

    xl, xc = x, ctx
    for l in range(DEPTH):
        update_ctx = l < DEPTH - 1
        mod_l = mod[l, :batch].reshape(batch, 6, D_MODEL)
        mod_c = jnp.broadcast_to(mod[l, batch].reshape(1, 6, D_MODEL), (batch, 6, D_MODEL))

        w_in_l = w_in[l]
        w_in_l = jnp.concatenate(
            [w_in_l[:, :_QB], w_in_l[:, _QB:_KB][:, perm], w_in_l[:, _KB:]], axis=1).astype(_BF16)
        w_o_l = w_o[l]
        w_o_l = jnp.concatenate(
            [w_o_l[:NA_W], w_o_l[NA_W:NA_W + WG_W][perm], w_o_l[NA_W + WG_W:]], axis=0).astype(_BF16)
        tile4 = lambda v: jnp.tile(v, MXU_DIM // HEAD_DIM)
        gains = _pad_rows(jnp.stack([tile4(qn_a[l]) * scale, tile4(kn_a[l]),
                                     tile4(qn_b[l]) * scale, tile4(kn_b[l])]), SUBLANES)
        bias = _na_bias_table(rpb_a[l], n_rows)
        conv_c_l = _pad_rows(conv_c[l], SUBLANES)
        g_attn_l = g_attn[l].reshape(1, D_MODEL)
        g_ffn_l = g_ffn[l].reshape(1, D_MODEL)
        wu = w_up[l].astype(_BF16)
        wd = w_down[l].astype(_BF16)
        taps = _pad_rows(conv_ffn[l], SUBLANES)
        sink_perm = sink_b[l] * LOG2E

        qa, ka, va, qb, kb, vb, cu, bg = _inproj_call(
            xl, mod_l, g_attn_l, w_in_l, gains, cos, sin, TOKEN_TILE)
        qa_c, ka_c, va_c, qb_c, kb_c, vb_c, cu_c, bg_c = _inproj_call(
            xc, mod_c, g_attn_l, w_in_l, gains, ones_c, zeros_c, c_len)
        ya = _na_call(qa, ka, va, ka_c, va_c, bias)
        yb = _wg_call(sink_perm, qb, kb, vb, kb_c, vb_c)
        xl = _tail_call(ya, yb, cu, bg, xl, mod_l, g_ffn_l, conv_c_l, w_o_l, wu, taps, wd,
                        TOKEN_TILE)
        if update_ctx:
            ya_c, yb_c = _ctx_attn_call(sink_perm, qa_c, ka_c, va_c, qb_c, kb_c, vb_c)
            xc = _tail_call(ya_c, yb_c, cu_c, bg_c, xc, mod_c, g_ffn_l, conv_c_l, w_o_l, wu, taps,
                            wd, c_len)
    return xl
```

```python
import functools
import math

import jax
import jax.numpy as jnp
import numpy as np
from jax import lax
from jax.experimental import pallas as pl
from jax.experimental.pallas import tpu as pltpu

D_MODEL = 1024
DEPTH = 2
GRID_W = 64
HEAD_DIM = 64
NA_HEADS = 6
NA_WIN_R = 8
NA_WIN_C = 16
WG_HEADS = 6
WG_KV_HEADS = 2
WG_WINDOW = 128
WG_BLOCK = 128
SC_CH = 256
CONV_W = 3
D_FF = 2816
ROPE_BASE = 10000.0
EPS = 1e-6
NEG = -1e30
LOG2E = math.log2(math.e)

NA_W = NA_HEADS * HEAD_DIM
WG_W = WG_HEADS * HEAD_DIM
WG_KV_W = WG_KV_HEADS * HEAD_DIM
IN_W = 3 * NA_W + WG_W + 2 * WG_KV_W + 3 * SC_CH

LANES = 128
SUBLANES = 8
MXU_DIM = 256
FF_CHUNK = MXU_DIM
N_FF_CHUNKS = D_FF // FF_CHUNK
TOKEN_TILE = 512
INPROJ_TILE = 1024
INPROJ_SUB_TILE = 256
NA_ROWS_PER_STEP = 8
SOFTMAX_ROWS = 32
WG_BLOCKS_PER_STEP = 4
VMEM_LIMIT = 48 * 1024 * 1024

_QA, _KA, _VA = 0, NA_W, 2 * NA_W
_QB = 3 * NA_W
_KB = _QB + WG_W
_VB = _KB + WG_KV_W
_U = _VB + WG_KV_W
_BG = _U + SC_CH
_CG = _BG + SC_CH

_F32 = jnp.float32
_BF16 = jnp.bfloat16


def _dot(a, b):
    return jnp.dot(a, b, preferred_element_type=_F32)


def _dot_nt(a, b):
    return lax.dot_general(a, b, (((1,), (1,)), ((), ())), preferred_element_type=_F32)


def _cparams(n_grid):
    return pltpu.CompilerParams(
        dimension_semantics=("arbitrary",) * n_grid, vmem_limit_bytes=VMEM_LIMIT)


def _ada_kernel(c_ref, w_ref, b_ref, o_ref):
    c = c_ref[...]
    s = c / (1.0 + jnp.exp(-c))
    s_hi = s.astype(_BF16)
    s_lo = (s - s_hi.astype(_F32)).astype(_BF16)
    w = w_ref[...]
    w_hi = w.astype(_BF16)
    w_lo = (w - w_hi.astype(_F32)).astype(_BF16)
    o_ref[...] = _dot(s_hi, w_hi) + _dot(s_lo, w_hi) + _dot(s_hi, w_lo) + b_ref[...]


def _ada_call(cvec, w_ada, b_ada):
    rows = cvec.shape[0]
    n_out = w_ada.shape[-1]
    nt = 1536
    return pl.pallas_call(
        _ada_kernel,
        grid=(DEPTH, n_out // nt),
        in_specs=[
            pl.BlockSpec((rows, D_MODEL), lambda l, j: (0, 0)),
            pl.BlockSpec((None, D_MODEL, nt), lambda l, j: (l, 0, j)),
            pl.BlockSpec((None, 1, nt), lambda l, j: (l, 0, j)),
        ],
        out_specs=pl.BlockSpec((None, rows, nt), lambda l, j: (l, 0, j)),
        out_shape=jax.ShapeDtypeStruct((DEPTH, rows, n_out), _F32),
        compiler_params=_cparams(2),
        name="adaln_mod",
    )(cvec, w_ada, b_ada.reshape(DEPTH, 1, n_out))


def _norm_mod(x, gain, shift):
    ms = jnp.mean(x * x, axis=-1, keepdims=True)
    return x * lax.rsqrt(ms + EPS) * gain + shift


def _lane_id(shape):
    return lax.broadcasted_iota(jnp.int32, shape, len(shape) - 1)


def _head_norm(x, gain):
    outs = []
    for j in range(x.shape[1] // LANES):
        xc = x[:, j * LANES:(j + 1) * LANES]
        sq = xc * xc
        low = _lane_id(xc.shape) < HEAD_DIM
        s_lo = jnp.sum(jnp.where(low, sq, 0.0), axis=-1, keepdims=True)
        s_hi = jnp.sum(jnp.where(low, 0.0, sq), axis=-1, keepdims=True)
        ms = jnp.where(low, s_lo, s_hi) * (1.0 / HEAD_DIM)
        outs.append(xc * lax.rsqrt(ms + EPS))
    return jnp.concatenate(outs, axis=1) * gain


def _inproj_kernel(x_ref, mod_ref, g_ref, w_ref, gain_ref, cos_ref, sin_ref,
                   qa_ref, ka_ref, va_ref, qb_ref, kb_ref, vb_ref, cu_ref, bg_ref, p_scr, *, sub):
    gain_qk_a = jnp.concatenate([gain_ref[0:1, :LANES], gain_ref[1:2, :LANES]], axis=1)
    gain_qk_b = jnp.concatenate([gain_ref[2:3, :LANES], gain_ref[3:4, :LANES]], axis=1)
    first_half = (_lane_id((sub, LANES)) % HEAD_DIM) < (HEAD_DIM // 2)
    norm_gain = g_ref[...] * (1.0 + mod_ref[1:2, :])
    for t in range(x_ref.shape[0] // sub):
        rows = slice(t * sub, (t + 1) * sub)
        h = _norm_mod(x_ref[rows, :], norm_gain, mod_ref[0:1, :]).astype(_BF16)
        p_scr[rows, :] = _dot(h, w_ref[...])

        p = p_scr[rows, _QA:_QA + MXU_DIM]
        qa_ref[rows, 0:MXU_DIM] = _head_norm(p,gain_ref[0:1, :]).astype(_BF16)
        p = p_scr[rows, _QA + MXU_DIM:_QA + 2 * MXU_DIM]
        pn = _head_norm(p,gain_qk_a).astype(_BF16)
        qa_ref[rows, MXU_DIM:NA_W] = pn[:, :LANES]
        ka_ref[rows, 0:LANES] = pn[:, LANES:]
        p = p_scr[rows, _QA + 2 * MXU_DIM:_QA + 3 * MXU_DIM]
        ka_ref[rows, LANES:NA_W] = _head_norm(p,gain_ref[1:2, :]).astype(_BF16)

        va_ref[rows, :] = p_scr[rows, _VA:_VA + NA_W].astype(_BF16)

        cos = cos_ref[rows, :]
        sin = sin_ref[rows, :]
        for k in range(2):
            p = p_scr[rows, _QB + k * MXU_DIM:_QB + (k + 1) * MXU_DIM]
            pn = _head_norm(p,gain_ref[2:3, :] if k == 0 else gain_qk_b)
            for j in range(2):
                xc = pn[:, j * LANES:(j + 1) * LANES]
                swapped = jnp.where(first_half, pltpu.roll(xc, LANES - HEAD_DIM // 2, 1),
                                    pltpu.roll(xc, HEAD_DIM // 2, 1))
                xr = (xc * cos + swapped * sin).astype(_BF16)
                col = k * MXU_DIM + j * LANES
                if col < WG_W:
                    qb_ref[rows, col:col + LANES] = xr
                else:
                    kb_ref[rows, :] = xr

        vb_ref[rows, :] = p_scr[rows, _VB:_VB + WG_KV_W].astype(_BF16)
        bg_ref[rows, :] = p_scr[rows, _BG:_BG + SC_CH]
        cu_ref[rows, :] = p_scr[rows, _CG:_CG + SC_CH] * p_scr[rows, _U:_U + SC_CH]


def _inproj_call(x, mod, g, w_in, gains, cos, sin, tile):
    b, s, _ = x.shape
    nt = s // tile
    tok = lambda width: pl.BlockSpec((None, tile, width), lambda bi, i: (bi, i, 0))
    const = lambda shape: pl.BlockSpec(shape, lambda bi, i: (0,) * len(shape),
                                       pipeline_mode=pl.Buffered(1))
    widths = (NA_W, NA_W, NA_W, WG_W, WG_KV_W, WG_KV_W, SC_CH, SC_CH)
    dtypes = (_BF16,) * 6 + (_F32, _F32)
    return pl.pallas_call(
        functools.partial(_inproj_kernel, sub=min(tile, INPROJ_SUB_TILE)),
        grid=(b, nt),
        in_specs=[
            tok(D_MODEL),
            pl.BlockSpec((None, 6, D_MODEL), lambda bi, i: (bi, 0, 0)),
            const((1, D_MODEL)),
            const((D_MODEL, IN_W)),
            const((SUBLANES, MXU_DIM)),
            pl.BlockSpec((tile, LANES), lambda bi, i: (i, 0)),
            pl.BlockSpec((tile, LANES), lambda bi, i: (i, 0)),
        ],
        out_specs=[tok(w) for w in widths],
        out_shape=[jax.ShapeDtypeStruct((b, s, w), d) for w, d in zip(widths, dtypes)],
        scratch_shapes=[pltpu.VMEM((tile, IN_W), _F32)],
        compiler_params=_cparams(2),
        name="mixer_inproj",
    )(x, mod, g, w_in, gains, cos, sin)


def _masked_halves(qp):
    qf = qp.astype(_F32)
    low = _lane_id(qf.shape) < HEAD_DIM
    return (jnp.where(low, qf, 0.0).astype(_BF16), jnp.where(low, 0.0, qf).astype(_BF16))


def _softmax_parts(parts, extra=None):
    m = parts[0].max(axis=-1, keepdims=True)
    for s in parts[1:]:
        m = jnp.maximum(m, s.max(axis=-1, keepdims=True))
    if extra is not None:
        m = jnp.maximum(m, extra)
    ps = [jnp.exp2(s - m) for s in parts]
    denom = ps[0].sum(axis=-1, keepdims=True)
    for p in ps[1:]:
        denom = denom + p.sum(axis=-1, keepdims=True)
    if extra is not None:
        denom = denom + jnp.exp2(extra - m)
    return [p.astype(_BF16) for p in ps], denom


def _na_kernel(q_ref, k_ref, v_ref, kc_ref, vc_ref, bias_ref, o_ref, *, rows_per_step, n_rows):
    step = pl.program_id(1)
    wr = min(NA_WIN_R, n_rows)
    low = _lane_id((GRID_W, LANES)) < HEAD_DIM
    offs, kstarts = [], []
    for rr in range(rows_per_step):
        r = step * rows_per_step + rr
        r0 = jnp.clip(r - wr // 2, 0, n_rows - wr)
        offs.append(r0 - r + NA_WIN_R - 1)
        kstarts.append(pl.multiple_of(r0 * GRID_W, GRID_W))

    for c in range(NA_HEADS // 2):
        cols = slice(c * LANES, (c + 1) * LANES)
        lhs = jnp.concatenate(
            [half for rr in range(rows_per_step)
             for half in _masked_halves(q_ref[rr * GRID_W:(rr + 1) * GRID_W, cols])], axis=0)
        s_ctx = _dot_nt(lhs, kc_ref[:, cols])
        p_ctx, o_loc, denoms = [], [], []
        for rr in range(rows_per_step):
            rows = slice(rr * 2 * GRID_W, (rr + 1) * 2 * GRID_W)
            kl = k_ref[pl.ds(kstarts[rr], wr * GRID_W), cols]
            vl = v_ref[pl.ds(kstarts[rr], wr * GRID_W), cols]
            bias = jnp.concatenate(
                [bias_ref[offs[rr], 2 * c], bias_ref[offs[rr], 2 * c + 1]], axis=0)
            s_loc = _dot_nt(lhs[rows], kl) + bias
            (pl_, pc_), dn = _softmax_parts([s_loc, s_ctx[rows]])
            o_loc.append(_dot(pl_, vl))
            p_ctx.append(pc_)
            denoms.append(dn)
        o_ctx = _dot(jnp.concatenate(p_ctx, axis=0), vc_ref[:, cols])
        for rr in range(rows_per_step):
            rows = slice(rr * 2 * GRID_W, (rr + 1) * 2 * GRID_W)
            o = (o_loc[rr] + o_ctx[rows]) / denoms[rr]
            o_ref[rr * GRID_W:(rr + 1) * GRID_W, cols] = jnp.where(
                low, o[:GRID_W], o[GRID_W:]).astype(_BF16)


def _na_call(qa, ka, va, kc, vc, bias):
    b, s, _ = qa.shape
    n_rows = s // GRID_W
    c_len = kc.shape[1]
    rps = NA_ROWS_PER_STEP
    blk = rps * GRID_W
    return pl.pallas_call(
        functools.partial(_na_kernel, rows_per_step=rps, n_rows=n_rows),
        grid=(b, n_rows // rps),
        in_specs=[
            pl.BlockSpec((None, blk, NA_W), lambda bi, j: (bi, j, 0)),
            pl.BlockSpec((None, s, NA_W), lambda bi, j: (bi, 0, 0)),
            pl.BlockSpec((None, s, NA_W), lambda bi, j: (bi, 0, 0)),
            pl.BlockSpec((None, c_len, NA_W), lambda bi, j: (bi, 0, 0)),
            pl.BlockSpec((None, c_len, NA_W), lambda bi, j: (bi, 0, 0)),
            pl.BlockSpec(bias.shape, lambda bi, j: (0, 0, 0, 0)),
        ],
        out_specs=pl.BlockSpec((None, blk, NA_W), lambda bi, j: (bi, j, 0)),
        out_shape=jax.ShapeDtypeStruct((b, s, NA_W), _BF16),
        compiler_params=_cparams(2),
        name="neighbourhood_attn",
    )(qa, ka, va, kc, vc, bias)


def _wg_window_start(blk, seq):
    span = 3 * WG_BLOCK
    return pl.multiple_of(jnp.clip((blk - 1) * WG_BLOCK, 0, seq - span), WG_BLOCK)


def _wg_scores(q_ref, k_ref, kc_ref, step, s_loc_scr, s_ctx_scr, *, seq, n_blk):
    span = 3 * WG_BLOCK
    per_blk = WG_HEADS * WG_BLOCK
    pieces = []
    for bb in range(n_blk):
        for c in range(WG_HEADS // 2):
            pieces.extend(_masked_halves(
                q_ref[bb * WG_BLOCK:(bb + 1) * WG_BLOCK, c * LANES:(c + 1) * LANES]))
    lhs = jnp.concatenate(pieces, axis=0)
    s_ctx_scr[...] = _dot_nt(lhs, kc_ref[...])
    for bb in range(n_blk):
        start = _wg_window_start(step * n_blk + bb, seq)
        s_loc_scr[bb * per_blk:(bb + 1) * per_blk, :] = _dot_nt(
            lhs[bb * per_blk:(bb + 1) * per_blk], k_ref[pl.ds(start, span), :])


def _wg_softmax_pv(sink_ref, s_loc_scr, s_ctx_scr, v_ref, vc_ref, step, o_ref,
                   p_loc_scr, p_ctx_scr, dn_scr, *, seq, n_blk):
    span = 3 * WG_BLOCK
    per_blk = WG_HEADS * WG_BLOCK
    low = _lane_id((WG_BLOCK, LANES)) < HEAD_DIM
    row_id = lax.broadcasted_iota(jnp.int32, (SOFTMAX_ROWS, span), 0)
    col_id = lax.broadcasted_iota(jnp.int32, (SOFTMAX_ROWS, span), 1)
    o_loc = []
    for bb in range(n_blk):
        blk = step * n_blk + bb
        start = _wg_window_start(blk, seq)
        for rc in range(WG_BLOCK // SOFTMAX_ROWS):
            dist = (blk * WG_BLOCK - start + rc * SOFTMAX_ROWS) + row_id - col_id
            mask_add = jnp.where(jnp.abs(dist) <= WG_WINDOW, 0.0, NEG).astype(_F32)
            for hb in range(WG_HEADS):
                head = (hb // 2) + (WG_HEADS // 2) * (hb % 2)
                r0 = bb * per_blk + hb * WG_BLOCK + rc * SOFTMAX_ROWS
                rows = slice(r0, r0 + SOFTMAX_ROWS)
                (pl_, pc_), dn = _softmax_parts(
                    [s_loc_scr[rows, :] + mask_add, s_ctx_scr[rows, :]], sink_ref[head])
                p_loc_scr[rows, :] = pl_
                p_ctx_scr[rows, :] = pc_
                dn_scr[rows, :] = jnp.broadcast_to(dn, (SOFTMAX_ROWS, LANES))
        o_loc.append(_dot(p_loc_scr[bb * per_blk:(bb + 1) * per_blk, :],
                          v_ref[pl.ds(start, span), :]))
    o_ctx = _dot(p_ctx_scr[...], vc_ref[...])
    for bb in range(n_blk):
        for c in range(WG_HEADS // 2):
            halves = []
            for half in range(2):
                hb = 2 * c + half
                rows = slice(hb * WG_BLOCK, (hb + 1) * WG_BLOCK)
                rows_all = slice(bb * per_blk + hb * WG_BLOCK, bb * per_blk + (hb + 1) * WG_BLOCK)
                halves.append((o_loc[bb][rows] + o_ctx[rows_all]) / dn_scr[rows_all, :])
            o_ref[bb * WG_BLOCK:(bb + 1) * WG_BLOCK, c * LANES:(c + 1) * LANES] = jnp.where(
                low, halves[0], halves[1]).astype(_BF16)


def _wg_kernel(sink_ref, q_ref, qn_ref, k_ref, kn_ref, v_ref, kc_ref, kcn_ref, vc_ref, o_ref,
               sa_loc, sa_ctx, sb_loc, sb_ctx, p_loc_scr, p_ctx_scr, dn_scr, *, seq, n_blk):
    bi, step = pl.program_id(0), pl.program_id(1)
    nxt = (step + 1) % pl.num_programs(1)
    scores = functools.partial(_wg_scores, seq=seq, n_blk=n_blk)
    finish = functools.partial(_wg_softmax_pv, sink_ref, seq=seq, n_blk=n_blk)

    @pl.when((bi == 0) & (step == 0))
    def _():
        scores(q_ref, k_ref, kc_ref, step, sa_loc, sa_ctx)

    @pl.when(step % 2 == 0)
    def _():
        scores(qn_ref, kn_ref, kcn_ref, nxt, sb_loc, sb_ctx)
        finish(sa_loc, sa_ctx, v_ref, vc_ref, step, o_ref, p_loc_scr, p_ctx_scr, dn_scr)

    @pl.when(step % 2 == 1)
    def _():
        scores(qn_ref, kn_ref, kcn_ref, nxt, sa_loc, sa_ctx)
        finish(sb_loc, sb_ctx, v_ref, vc_ref, step, o_ref, p_loc_scr, p_ctx_scr, dn_scr)


def _next_step_maps(n_batch, n_steps):
    nb = lambda bi, i: jnp.minimum(bi + (i + 1) // n_steps, n_batch - 1)
    return (lambda bi, i: (nb(bi, i), (i + 1) % n_steps, 0)), (lambda bi, i: (nb(bi, i), 0, 0))


def _wg_call(sink, qb, kb, vb, kc, vc):
    b, s, _ = qb.shape
    c_len = kc.shape[1]
    n_blk = WG_BLOCKS_PER_STEP
    q_rows = n_blk * WG_BLOCK
    n_steps = s // q_rows
    assert n_steps % 2 == 0
    next_q, next_kv = _next_step_maps(b, n_steps)
    cur_q = lambda bi, i: (bi, i, 0)
    cur_kv = lambda bi, i: (bi, 0, 0)
    s_rows = n_blk * WG_HEADS * WG_BLOCK
    return pl.pallas_call(
        functools.partial(_wg_kernel, seq=s, n_blk=n_blk),
        grid=(b, n_steps),
        in_specs=[
            pl.BlockSpec(memory_space=pltpu.SMEM),
            pl.BlockSpec((None, q_rows, WG_W), cur_q),
            pl.BlockSpec((None, q_rows, WG_W), next_q),
            pl.BlockSpec((None, s, WG_KV_W), cur_kv),
            pl.BlockSpec((None, s, WG_KV_W), next_kv),
            pl.BlockSpec((None, s, WG_KV_W), cur_kv),
            pl.BlockSpec((None, c_len, WG_KV_W), cur_kv),
            pl.BlockSpec((None, c_len, WG_KV_W), next_kv),
            pl.BlockSpec((None, c_len, WG_KV_W), cur_kv),
        ],
        out_specs=pl.BlockSpec((None, q_rows, WG_W), cur_q),
        out_shape=jax.ShapeDtypeStruct((b, s, WG_W), _BF16),
        scratch_shapes=[pltpu.VMEM((s_rows, 3 * WG_BLOCK), _F32), pltpu.VMEM((s_rows, c_len), _F32),
                        pltpu.VMEM((s_rows, 3 * WG_BLOCK), _F32), pltpu.VMEM((s_rows, c_len), _F32),
                        pltpu.VMEM((s_rows, 3 * WG_BLOCK), _BF16), pltpu.VMEM((s_rows, c_len), _BF16),
                        pltpu.VMEM((s_rows, LANES), _F32)],
        compiler_params=_cparams(2),
        name="window_gqa",
    )(sink, qb, qb, kb, kb, vb, kc, kc, vc)


def _ctx_attn_kernel(sink_ref, qa_ref, ka_ref, va_ref, qb_ref, kb_ref, vb_ref, ya_ref, yb_ref):
    c_len = qa_ref.shape[0]
    low = _lane_id((c_len, LANES)) < HEAD_DIM
    for c in range(NA_HEADS // 2):
        cols = slice(c * LANES, (c + 1) * LANES)
        lhs = jnp.concatenate(_masked_halves(qa_ref[:, cols]), axis=0)
        s = _dot_nt(lhs, ka_ref[:, cols])
        (p,), denom = _softmax_parts([s])
        o = _dot(p, va_ref[:, cols]) / denom
        ya_ref[:, cols] = jnp.where(low, o[:c_len], o[c_len:]).astype(_BF16)
    kb = kb_ref[...]
    vb = vb_ref[...]
    for c in range(WG_HEADS // 2):
        cols = slice(c * LANES, (c + 1) * LANES)
        halves = _masked_halves(qb_ref[:, cols])
        outs = []
        for half in range(2):
            head = c + (WG_HEADS // 2) * half
            s = _dot_nt(halves[half], kb)
            (p,), denom = _softmax_parts([s], sink_ref[head])
            outs.append(_dot(p, vb) / denom)
        yb_ref[:, cols] = jnp.where(low, outs[0], outs[1]).astype(_BF16)


def _ctx_attn_call(sink, qa, ka, va, qb, kb, vb):
    b, c_len, _ = qa.shape
    spec = lambda w: pl.BlockSpec((None, c_len, w), lambda bi: (bi, 0, 0))
    return pl.pallas_call(
        _ctx_attn_kernel,
        grid=(b,),
        in_specs=[pl.BlockSpec(memory_space=pltpu.SMEM), spec(NA_W), spec(NA_W), spec(NA_W),
                  spec(WG_W), spec(WG_KV_W), spec(WG_KV_W)],
        out_specs=[spec(NA_W), spec(WG_W)],
        out_shape=[jax.ShapeDtypeStruct((b, c_len, NA_W), _BF16),
                   jax.ShapeDtypeStruct((b, c_len, WG_W), _BF16)],
        compiler_params=_cparams(1),
        name="context_attn",
    )(sink, qa, ka, va, qb, kb, vb)


HALO = 16


def _shift_rows(ext, lo, n):
    total = ext.shape[0]
    prev = pltpu.roll(ext, 1, 0)[lo:lo + n]
    nxt = pltpu.roll(ext, total - 1, 0)[lo:lo + n]
    return prev, nxt


def _tail_kernel(ya_ref, yap_ref, yan_ref, yb_ref, ybp_ref, ybn_ref, cu_ref, cup_ref, cun_ref,
                 bg_ref, bgp_ref, bgn_ref, x_ref, xp_ref, xn_ref, mod_ref, g_ref, convc_ref, wo_ref,
                 wu_ref, convf_ref, wd_ref, o_ref, xm_scr, h_scr, act_scr):
    i = pl.program_id(1)
    tile = x_ref.shape[0]
    ext_rows = tile + 2 * HALO
    prev_ok = (i > 0).astype(_F32)
    next_ok = (i < pl.num_programs(1) - 1).astype(_F32)
    cat = lambda p, m, n: jnp.concatenate([p, m, n], axis=0)

    cu = cat(cup_ref[...] * prev_ok, cu_ref[...], cun_ref[...] * next_ok)
    left, right = _shift_rows(cu, 0, ext_rows)
    yc = cat(bgp_ref[...], bg_ref[...], bgn_ref[...]) * (
        left * convc_ref[0:1, :] + cu * convc_ref[1:2, :] + right * convc_ref[2:3, :])
    ya = cat(yap_ref[...], ya_ref[...], yan_ref[...])
    yb = cat(ybp_ref[...], yb_ref[...], ybn_ref[...])
    yc = yc.astype(_BF16)
    x_ext = cat(xp_ref[...], x_ref[...], xn_ref[...])
    norm_gain = g_ref[...] * (1.0 + mod_ref[4:5, :])
    half = ext_rows // 2
    for hh in range(2):
        rows = slice(hh * half, (hh + 1) * half)
        y = (_dot(ya[rows], wo_ref[0:NA_W, :]) + _dot(yb[rows], wo_ref[NA_W:NA_W + WG_W, :])
             + _dot(yc[rows], wo_ref[NA_W + WG_W:, :]))
        xm = x_ext[rows] + mod_ref[2:3, :] * y
        xm_scr[rows, :] = xm
        row = hh * half + lax.broadcasted_iota(jnp.int32, (half, 1), 0)
        inside = jnp.where(row < HALO, prev_ok, jnp.where(row >= HALO + tile, next_ok, 1.0))
        h_scr[rows, :] = (_norm_mod(xm, norm_gain, mod_ref[3:4, :]) * inside).astype(_BF16)

    for c in range(N_FF_CHUNKS):
        a_cols = slice(c * FF_CHUNK, (c + 1) * FF_CHUNK)
        g_cols = slice(D_FF + c * FF_CHUNK, D_FF + (c + 1) * FF_CHUNK)
        h = h_scr[...]
        ua = _dot(h, wu_ref[:, a_cols])
        ug = _dot(h, wu_ref[:, g_cols])
        pa, na = _shift_rows(ua, HALO, tile)
        pg, ng = _shift_rows(ug, HALO, tile)
        mid = slice(HALO, HALO + tile)
        a = (pa * convf_ref[0:1, a_cols] + ua[mid] * convf_ref[1:2, a_cols]
             + na * convf_ref[2:3, a_cols])
        gt = (pg * convf_ref[0:1, g_cols] + ug[mid] * convf_ref[1:2, g_cols]
              + ng * convf_ref[2:3, g_cols])
        act_scr[:, a_cols] = ((a / (1.0 + jnp.exp(-a))) * gt).astype(_BF16)

    o_ref[...] = xm_scr[HALO:HALO + tile, :] + mod_ref[5:6, :] * _dot(act_scr[...], wd_ref[...])


def _tile_and_halo_specs(tile, width, n_rows):
    per = tile // HALO
    last = n_rows // HALO - 1
    main = pl.BlockSpec((None, tile, width), lambda bi, i: (bi, i, 0))
    prev = pl.BlockSpec((None, HALO, width), lambda bi, i: (bi, jnp.maximum(i * per - 1, 0), 0))
    nxt = pl.BlockSpec((None, HALO, width), lambda bi, i: (bi, jnp.minimum((i + 1) * per, last), 0))
    return [main, prev, nxt]


def _tail_call(ya, yb, cu, bg, x, mod, g, conv_c, w_o, wu, conv_f, wd, tile):
    b, s, _ = x.shape
    const = lambda shape: pl.BlockSpec(shape, lambda bi, i: (0,) * len(shape),
                                       pipeline_mode=pl.Buffered(1))
    in_specs = []
    for width in (NA_W, WG_W, SC_CH, SC_CH, D_MODEL):
        in_specs += _tile_and_halo_specs(tile, width, s)
    in_specs += [pl.BlockSpec((None, 6, D_MODEL), lambda bi, i: (bi, 0, 0)),
                 const((1, D_MODEL)), const(conv_c.shape), const(w_o.shape),
                 const(wu.shape), const(conv_f.shape), const(wd.shape)]
    return pl.pallas_call(
        _tail_kernel,
        grid=(b, s // tile),
        in_specs=in_specs,
        out_specs=pl.BlockSpec((None, tile, D_MODEL), lambda bi, i: (bi, i, 0)),
        out_shape=jax.ShapeDtypeStruct((b, s, D_MODEL), _F32),
        scratch_shapes=[pltpu.VMEM((tile + 2 * HALO, D_MODEL), _F32),
                        pltpu.VMEM((tile + 2 * HALO, D_MODEL), _BF16),
                        pltpu.VMEM((tile, D_FF), _BF16)],
        compiler_params=_cparams(2),
        name="layer_tail",
    )(ya, ya, ya, yb, yb, yb, cu, cu, cu, bg, bg, bg, x, x, x, mod, g, conv_c, w_o, wu, conv_f, wd)


def _wg_head_blocks(a, axis):
    order = []
    for c in range(WG_HEADS // 2):
        order += [c, c + WG_HEADS // 2]
    return [lax.slice_in_dim(a, h * HEAD_DIM, (h + 1) * HEAD_DIM, axis=axis) for h in order]


def _rope_tables(seq):
    t = np.arange(seq)
    row = (t // GRID_W).astype(np.float64)
    col = (t % GRID_W).astype(np.float64)
    half = HEAD_DIM // 2
    n_freq = half // 2
    inv = ROPE_BASE ** (-np.arange(n_freq, dtype=np.float64) / n_freq)
    ang = np.concatenate([row[:, None] * inv, col[:, None] * inv], axis=-1)
    cos = np.concatenate([np.cos(ang), np.cos(ang)], axis=-1)
    sin = np.concatenate([-np.sin(ang), np.sin(ang)], axis=-1)
    reps = LANES // HEAD_DIM
    return (jnp.asarray(np.tile(cos, (1, reps)), _F32), jnp.asarray(np.tile(sin, (1, reps)), _F32))


def _na_bias_table(rpb, n_rows):
    wr = min(NA_WIN_R, n_rows)
    q = np.arange(GRID_W)
    c0 = np.clip(q - NA_WIN_C // 2, 0, GRID_W - NA_WIN_C)
    kc = np.arange(GRID_W)
    ok = (kc[None, :] >= c0[:, None]) & (kc[None, :] < c0[:, None] + NA_WIN_C)
    pad = GRID_W - NA_WIN_C
    padded = jnp.pad(rpb.astype(_F32), ((0, 0), (0, 0), (pad, pad)))
    toep = jnp.stack([padded[:, :, GRID_W - 1 - qi:2 * GRID_W - 1 - qi] for qi in range(GRID_W)],
                     axis=1)
    toep = jnp.where(ok[None, :, None, :], toep * LOG2E, NEG)
    n_d = 2 * NA_WIN_R - 1
    toep = toep.reshape(NA_HEADS, GRID_W, n_d * GRID_W)
    return jnp.stack([toep[:, :, off * GRID_W:(off + wr) * GRID_W] for off in range(NA_WIN_R)],
                     axis=0)


def _pad_rows(a, rows):
    return jnp.concatenate([a, jnp.zeros((rows - a.shape[0],) + a.shape[1:], a.dtype)], axis=0)


def kernel(x, c, ctx, c_ctx, w_ada, b_ada, g_attn, w_in, qn_a, kn_a, qn_b, kn_b, rpb_a, sink_b,
           conv_c, w_o, g_ffn, w_up, conv_ffn, w_down):
    batch, seq, _ = x.shape
    c_len = ctx.shape[1]
    n_rows = seq // GRID_W
    scale = LOG2E / math.sqrt(HEAD_DIM)

    cvec = _pad_rows(jnp.concatenate([c, c_ctx[None, :]], axis=0), 2 * SUBLANES)
    mod = _ada_call(cvec, w_ada, b_ada)

    cos, sin = _rope_tables(seq)
    ones_c = jnp.ones((c_len, LANES), _F32)
    zeros_c = jnp.zeros((c_len, LANES), _F32)

    xl, xc = x, ctx
    for l in range(DEPTH):
        update_ctx = l < DEPTH - 1
        mod_l = mod[l, :batch].reshape(batch, 6, D_MODEL)
        mod_c = jnp.broadcast_to(mod[l, batch].reshape(1, 6, D_MODEL), (batch, 6, D_MODEL))

        w_in_l = w_in[l]
        w_in_l = jnp.concatenate(
            [w_in_l[:, :_QB]] + _wg_head_blocks(w_in_l[:, _QB:_KB], 1) + [w_in_l[:, _KB:]],
            axis=1).astype(_BF16)
        w_o_l = w_o[l]
        w_o_l = jnp.concatenate(
            [w_o_l[:NA_W]] + _wg_head_blocks(w_o_l[NA_W:NA_W + WG_W], 0) + [w_o_l[NA_W + WG_W:]],
            axis=0).astype(_BF16)
        tile4 = lambda v: jnp.tile(v, MXU_DIM // HEAD_DIM)
        gains = _pad_rows(jnp.stack([tile4(qn_a[l]) * scale, tile4(kn_a[l]),
                                     tile4(qn_b[l]) * scale, tile4(kn_b[l])]), SUBLANES)
        bias = _na_bias_table(rpb_a[l], n_rows)
        conv_c_l = _pad_rows(conv_c[l], SUBLANES)
        g_attn_l = g_attn[l].reshape(1, D_MODEL)
        g_ffn_l = g_ffn[l].reshape(1, D_MODEL)
        wu = w_up[l].astype(_BF16)
        wd = w_down[l].astype(_BF16)
        taps = _pad_rows(conv_ffn[l], SUBLANES)
        sink_perm = sink_b[l] * LOG2E

        qa, ka, va, qb, kb, vb, cu, bg = _inproj_call(
            xl, mod_l, g_attn_l, w_in_l, gains, cos, sin, INPROJ_TILE)
        qa_c, ka_c, va_c, qb_c, kb_c, vb_c, cu_c, bg_c = _inproj_call(
            xc, mod_c, g_attn_l, w_in_l, gains, ones_c, zeros_c, c_len)
        ya = _na_call(qa, ka, va, ka_c, va_c, bias)
        yb = _wg_call(sink_perm, qb, kb, vb, kb_c, vb_c)
        xl = _tail_call(ya, yb, cu, bg, xl, mod_l, g_ffn_l, conv_c_l, w_o_l, wu, taps, wd,
                        TOKEN_TILE)
        if update_ctx:
            ya_c, yb_c = _ctx_attn_call(sink_perm, qa_c, ka_c, va_c, qb_c, kb_c, vb_c)
            xc = _tail_call(ya_c, yb_c, cu_c, bg_c, xc, mod_c, g_ffn_l, conv_c_l, w_o_l, wu, taps,
                            wd, c_len)
    return xl
```

```python
import functools
import math

import jax
import jax.numpy as jnp
import numpy as np
from jax import lax
from jax.experimental import pallas as pl
from jax.experimental.pallas import tpu as pltpu

D_MODEL = 1024
DEPTH = 2
GRID_W = 64
HEAD_DIM = 64
NA_HEADS = 6
NA_WIN_R = 8
NA_WIN_C = 16
WG_HEADS = 6
WG_KV_HEADS = 2
WG_WINDOW = 128
WG_BLOCK = 128
SC_CH = 256
CONV_W = 3
D_FF = 2816
ROPE_BASE = 10000.0
EPS = 1e-6
NEG = -1e30
LOG2E = math.log2(math.e)

NA_W = NA_HEADS * HEAD_DIM
WG_W = WG_HEADS * HEAD_DIM
WG_KV_W = WG_KV_HEADS * HEAD_DIM
IN_W = 3 * NA_W + WG_W + 2 * WG_KV_W + 3 * SC_CH

LANES = 128
SUBLANES = 8
MXU_DIM = 256
FF_CHUNK = MXU_DIM
N_FF_CHUNKS = D_FF // FF_CHUNK
TOKEN_TILE = 512
INPROJ_TILE = 1024
INPROJ_SUB_TILE = 256
NA_ROWS_PER_STEP = 16
SOFTMAX_ROWS = 32
WG_BLOCKS_PER_STEP = 4
VMEM_LIMIT = 48 * 1024 * 1024

_QA, _KA, _VA = 0, NA_W, 2 * NA_W
_QB = 3 * NA_W
_KB = _QB + WG_W
_VB = _KB + WG_KV_W
_U = _VB + WG_KV_W
_BG = _U + SC_CH
_CG = _BG + SC_CH

_F32 = jnp.float32
_BF16 = jnp.bfloat16


def _dot(a, b):
    return jnp.dot(a, b, preferred_element_type=_F32)


def _dot_nt(a, b):
    return lax.dot_general(a, b, (((1,), (1,)), ((), ())), preferred_element_type=_F32)


def _cparams(n_grid):
    return pltpu.CompilerParams(
        dimension_semantics=("arbitrary",) * n_grid, vmem_limit_bytes=VMEM_LIMIT)


def _ada_kernel(c_ref, w_ref, b_ref, o_ref):
    c = c_ref[...]
    s = c / (1.0 + jnp.exp(-c))
    s_hi = s.astype(_BF16)
    s_lo = (s - s_hi.astype(_F32)).astype(_BF16)
    w = w_ref[...]
    w_hi = w.astype(_BF16)
    w_lo = (w - w_hi.astype(_F32)).astype(_BF16)
    o_ref[...] = _dot(s_hi, w_hi) + _dot(s_lo, w_hi) + _dot(s_hi, w_lo) + b_ref[...]


def _ada_call(cvec, w_ada, b_ada):
    rows = cvec.shape[0]
    n_out = w_ada.shape[-1]
    nt = 1536
    return pl.pallas_call(
        _ada_kernel,
        grid=(DEPTH, n_out // nt),
        in_specs=[
            pl.BlockSpec((rows, D_MODEL), lambda l, j: (0, 0)),
            pl.BlockSpec((None, D_MODEL, nt), lambda l, j: (l, 0, j)),
            pl.BlockSpec((None, 1, nt), lambda l, j: (l, 0, j)),
        ],
        out_specs=pl.BlockSpec((None, rows, nt), lambda l, j: (l, 0, j)),
        out_shape=jax.ShapeDtypeStruct((DEPTH, rows, n_out), _F32),
        compiler_params=_cparams(2),
        name="adaln_mod",
    )(cvec, w_ada, b_ada.reshape(DEPTH, 1, n_out))


def _norm_mod(x, gain, shift):
    ms = jnp.mean(x * x, axis=-1, keepdims=True)
    return x * lax.rsqrt(ms + EPS) * gain + shift


def _lane_id(shape):
    return lax.broadcasted_iota(jnp.int32, shape, len(shape) - 1)


def _head_norm(x, gain):
    outs = []
    for j in range(x.shape[1] // LANES):
        xc = x[:, j * LANES:(j + 1) * LANES]
        sq = xc * xc
        low = _lane_id(xc.shape) < HEAD_DIM
        s_lo = jnp.sum(jnp.where(low, sq, 0.0), axis=-1, keepdims=True)
        s_hi = jnp.sum(jnp.where(low, 0.0, sq), axis=-1, keepdims=True)
        ms = jnp.where(low, s_lo, s_hi) * (1.0 / HEAD_DIM)
        outs.append(xc * lax.rsqrt(ms + EPS))
    return jnp.concatenate(outs, axis=1) * gain


def _inproj_kernel(x_ref, mod_ref, g_ref, w_ref, gain_ref, cos_ref, sin_ref,
                   qa_ref, ka_ref, va_ref, qb_ref, kb_ref, vb_ref, cu_ref, bg_ref, p_scr, *, sub):
    gain_qk_a = jnp.concatenate([gain_ref[0:1, :LANES], gain_ref[1:2, :LANES]], axis=1)
    gain_qk_b = jnp.concatenate([gain_ref[2:3, :LANES], gain_ref[3:4, :LANES]], axis=1)
    first_half = (_lane_id((sub, LANES)) % HEAD_DIM) < (HEAD_DIM // 2)
    norm_gain = g_ref[...] * (1.0 + mod_ref[1:2, :])
    for t in range(x_ref.shape[0] // sub):
        rows = slice(t * sub, (t + 1) * sub)
        h = _norm_mod(x_ref[rows, :], norm_gain, mod_ref[0:1, :]).astype(_BF16)
        p_scr[rows, :] = _dot(h, w_ref[...])

        p = p_scr[rows, _QA:_QA + MXU_DIM]
        qa_ref[rows, 0:MXU_DIM] = _head_norm(p,gain_ref[0:1, :]).astype(_BF16)
        p = p_scr[rows, _QA + MXU_DIM:_QA + 2 * MXU_DIM]
        pn = _head_norm(p,gain_qk_a).astype(_BF16)
        qa_ref[rows, MXU_DIM:NA_W] = pn[:, :LANES]
        ka_ref[rows, 0:LANES] = pn[:, LANES:]
        p = p_scr[rows, _QA + 2 * MXU_DIM:_QA + 3 * MXU_DIM]
        ka_ref[rows, LANES:NA_W] = _head_norm(p,gain_ref[1:2, :]).astype(_BF16)

        va_ref[rows, :] = p_scr[rows, _VA:_VA + NA_W].astype(_BF16)

        cos = cos_ref[rows, :]
        sin = sin_ref[rows, :]
        for k in range(2):
            p = p_scr[rows, _QB + k * MXU_DIM:_QB + (k + 1) * MXU_DIM]
            pn = _head_norm(p,gain_ref[2:3, :] if k == 0 else gain_qk_b)
            for j in range(2):
                xc = pn[:, j * LANES:(j + 1) * LANES]
                swapped = jnp.where(first_half, pltpu.roll(xc, LANES - HEAD_DIM // 2, 1),
                                    pltpu.roll(xc, HEAD_DIM // 2, 1))
                xr = (xc * cos + swapped * sin).astype(_BF16)
                col = k * MXU_DIM + j * LANES
                if col < WG_W:
                    qb_ref[rows, col:col + LANES] = xr
                else:
                    kb_ref[rows, :] = xr

        vb_ref[rows, :] = p_scr[rows, _VB:_VB + WG_KV_W].astype(_BF16)
        bg_ref[rows, :] = p_scr[rows, _BG:_BG + SC_CH]
        cu_ref[rows, :] = p_scr[rows, _CG:_CG + SC_CH] * p_scr[rows, _U:_U + SC_CH]


def _inproj_call(x, mod, g, w_in, gains, cos, sin, tile):
    b, s, _ = x.shape
    nt = s // tile
    tok = lambda width: pl.BlockSpec((None, tile, width), lambda bi, i: (bi, i, 0))
    const = lambda shape: pl.BlockSpec(shape, lambda bi, i: (0,) * len(shape),
                                       pipeline_mode=pl.Buffered(1))
    widths = (NA_W, NA_W, NA_W, WG_W, WG_KV_W, WG_KV_W, SC_CH, SC_CH)
    dtypes = (_BF16,) * 6 + (_F32, _F32)
    return pl.pallas_call(
        functools.partial(_inproj_kernel, sub=min(tile, INPROJ_SUB_TILE)),
        grid=(b, nt),
        in_specs=[
            tok(D_MODEL),
            pl.BlockSpec((None, 6, D_MODEL), lambda bi, i: (bi, 0, 0)),
            const((1, D_MODEL)),
            const((D_MODEL, IN_W)),
            const((SUBLANES, MXU_DIM)),
            pl.BlockSpec((tile, LANES), lambda bi, i: (i, 0)),
            pl.BlockSpec((tile, LANES), lambda bi, i: (i, 0)),
        ],
        out_specs=[tok(w) for w in widths],
        out_shape=[jax.ShapeDtypeStruct((b, s, w), d) for w, d in zip(widths, dtypes)],
        scratch_shapes=[pltpu.VMEM((tile, IN_W), _F32)],
        compiler_params=_cparams(2),
        name="mixer_inproj",
    )(x, mod, g, w_in, gains, cos, sin)


def _masked_halves(qp):
    qf = qp.astype(_F32)
    low = _lane_id(qf.shape) < HEAD_DIM
    return (jnp.where(low, qf, 0.0).astype(_BF16), jnp.where(low, 0.0, qf).astype(_BF16))


def _softmax_parts(parts, extra=None):
    m = parts[0].max(axis=-1, keepdims=True)
    for s in parts[1:]:
        m = jnp.maximum(m, s.max(axis=-1, keepdims=True))
    if extra is not None:
        m = jnp.maximum(m, extra)
    ps = [jnp.exp2(s - m) for s in parts]
    denom = ps[0].sum(axis=-1, keepdims=True)
    for p in ps[1:]:
        denom = denom + p.sum(axis=-1, keepdims=True)
    if extra is not None:
        denom = denom + jnp.exp2(extra - m)
    return [p.astype(_BF16) for p in ps], denom


def _na_kernel(q_ref, k_ref, v_ref, kc_ref, vc_ref, bias_ref, o_ref, *, rows_per_step, n_rows):
    step = pl.program_id(1)
    wr = min(NA_WIN_R, n_rows)
    low = _lane_id((GRID_W, LANES)) < HEAD_DIM
    offs, kstarts = [], []
    for rr in range(rows_per_step):
        r = step * rows_per_step + rr
        r0 = jnp.clip(r - wr // 2, 0, n_rows - wr)
        offs.append(r0 - r + NA_WIN_R - 1)
        kstarts.append(pl.multiple_of(r0 * GRID_W, GRID_W))

    for c in range(NA_HEADS // 2):
        cols = slice(c * LANES, (c + 1) * LANES)
        lhs = jnp.concatenate(
            [half for rr in range(rows_per_step)
             for half in _masked_halves(q_ref[rr * GRID_W:(rr + 1) * GRID_W, cols])], axis=0)
        s_ctx = _dot_nt(lhs, kc_ref[:, cols])
        p_ctx, o_loc, denoms = [], [], []
        for rr in range(rows_per_step):
            rows = slice(rr * 2 * GRID_W, (rr + 1) * 2 * GRID_W)
            kl = k_ref[pl.ds(kstarts[rr], wr * GRID_W), cols]
            vl = v_ref[pl.ds(kstarts[rr], wr * GRID_W), cols]
            bias = jnp.concatenate(
                [bias_ref[offs[rr], 2 * c], bias_ref[offs[rr], 2 * c + 1]], axis=0)
            s_loc = _dot_nt(lhs[rows], kl) + bias
            (pl_, pc_), dn = _softmax_parts([s_loc, s_ctx[rows]])
            o_loc.append(_dot(pl_, vl))
            p_ctx.append(pc_)
            denoms.append(dn)
        o_ctx = _dot(jnp.concatenate(p_ctx, axis=0), vc_ref[:, cols])
        for rr in range(rows_per_step):
            rows = slice(rr * 2 * GRID_W, (rr + 1) * 2 * GRID_W)
            o = (o_loc[rr] + o_ctx[rows]) / denoms[rr]
            o_ref[rr * GRID_W:(rr + 1) * GRID_W, cols] = jnp.where(
                low, o[:GRID_W], o[GRID_W:]).astype(_BF16)


def _na_call(qa, ka, va, kc, vc, bias):
    b, s, _ = qa.shape
    n_rows = s // GRID_W
    c_len = kc.shape[1]
    rps = NA_ROWS_PER_STEP
    blk = rps * GRID_W
    return pl.pallas_call(
        functools.partial(_na_kernel, rows_per_step=rps, n_rows=n_rows),
        grid=(b, n_rows // rps),
        in_specs=[
            pl.BlockSpec((None, blk, NA_W), lambda bi, j: (bi, j, 0)),
            pl.BlockSpec((None, s, NA_W), lambda bi, j: (bi, 0, 0)),
            pl.BlockSpec((None, s, NA_W), lambda bi, j: (bi, 0, 0)),
            pl.BlockSpec((None, c_len, NA_W), lambda bi, j: (bi, 0, 0)),
            pl.BlockSpec((None, c_len, NA_W), lambda bi, j: (bi, 0, 0)),
            pl.BlockSpec(bias.shape, lambda bi, j: (0, 0, 0, 0)),
        ],
        out_specs=pl.BlockSpec((None, blk, NA_W), lambda bi, j: (bi, j, 0)),
        out_shape=jax.ShapeDtypeStruct((b, s, NA_W), _BF16),
        compiler_params=_cparams(2),
        name="neighbourhood_attn",
    )(qa, ka, va, kc, vc, bias)


def _wg_window_start(blk, seq):
    span = 3 * WG_BLOCK
    return pl.multiple_of(jnp.clip((blk - 1) * WG_BLOCK, 0, seq - span), WG_BLOCK)


def _wg_scores(q_ref, k_ref, kc_ref, step, s_loc_scr, s_ctx_scr, *, seq, n_blk):
    span = 3 * WG_BLOCK
    per_blk = WG_HEADS * WG_BLOCK
    pieces = []
    for bb in range(n_blk):
        for c in range(WG_HEADS // 2):
            pieces.extend(_masked_halves(
                q_ref[bb * WG_BLOCK:(bb + 1) * WG_BLOCK, c * LANES:(c + 1) * LANES]))
    lhs = jnp.concatenate(pieces, axis=0)
    s_ctx_scr[...] = _dot_nt(lhs, kc_ref[...])
    for bb in range(n_blk):
        start = _wg_window_start(step * n_blk + bb, seq)
        s_loc_scr[bb * per_blk:(bb + 1) * per_blk, :] = _dot_nt(
            lhs[bb * per_blk:(bb + 1) * per_blk], k_ref[pl.ds(start, span), :])


def _wg_softmax_pv(sink_ref, s_loc_scr, s_ctx_scr, v_ref, vc_ref, step, o_ref,
                   p_loc_scr, p_ctx_scr, dn_scr, *, seq, n_blk):
    span = 3 * WG_BLOCK
    per_blk = WG_HEADS * WG_BLOCK
    low = _lane_id((WG_BLOCK, LANES)) < HEAD_DIM
    row_id = lax.broadcasted_iota(jnp.int32, (SOFTMAX_ROWS, span), 0)
    col_id = lax.broadcasted_iota(jnp.int32, (SOFTMAX_ROWS, span), 1)
    o_loc = []
    for bb in range(n_blk):
        blk = step * n_blk + bb
        start = _wg_window_start(blk, seq)
        for rc in range(WG_BLOCK // SOFTMAX_ROWS):
            dist = (blk * WG_BLOCK - start + rc * SOFTMAX_ROWS) + row_id - col_id
            mask_add = jnp.where(jnp.abs(dist) <= WG_WINDOW, 0.0, NEG).astype(_F32)
            for hb in range(WG_HEADS):
                head = (hb // 2) + (WG_HEADS // 2) * (hb % 2)
                r0 = bb * per_blk + hb * WG_BLOCK + rc * SOFTMAX_ROWS
                rows = slice(r0, r0 + SOFTMAX_ROWS)
                (pl_, pc_), dn = _softmax_parts(
                    [s_loc_scr[rows, :] + mask_add, s_ctx_scr[rows, :]], sink_ref[head])
                p_loc_scr[rows, :] = pl_
                p_ctx_scr[rows, :] = pc_
                dn_scr[rows, :] = jnp.broadcast_to(dn, (SOFTMAX_ROWS, LANES))
        o_loc.append(_dot(p_loc_scr[bb * per_blk:(bb + 1) * per_blk, :],
                          v_ref[pl.ds(start, span), :]))
    o_ctx = _dot(p_ctx_scr[...], vc_ref[...])
    for bb in range(n_blk):
        for c in range(WG_HEADS // 2):
            halves = []
            for half in range(2):
                hb = 2 * c + half
                rows = slice(hb * WG_BLOCK, (hb + 1) * WG_BLOCK)
                rows_all = slice(bb * per_blk + hb * WG_BLOCK, bb * per_blk + (hb + 1) * WG_BLOCK)
                halves.append((o_loc[bb][rows] + o_ctx[rows_all]) / dn_scr[rows_all, :])
            o_ref[bb * WG_BLOCK:(bb + 1) * WG_BLOCK, c * LANES:(c + 1) * LANES] = jnp.where(
                low, halves[0], halves[1]).astype(_BF16)


def _wg_kernel(sink_ref, q_ref, qn_ref, k_ref, kn_ref, v_ref, kc_ref, kcn_ref, vc_ref, o_ref,
               sa_loc, sa_ctx, sb_loc, sb_ctx, p_loc_scr, p_ctx_scr, dn_scr, *, seq, n_blk):
    bi, step = pl.program_id(0), pl.program_id(1)
    nxt = (step + 1) % pl.num_programs(1)
    scores = functools.partial(_wg_scores, seq=seq, n_blk=n_blk)
    finish = functools.partial(_wg_softmax_pv, sink_ref, seq=seq, n_blk=n_blk)

    @pl.when((bi == 0) & (step == 0))
    def _():
        scores(q_ref, k_ref, kc_ref, step, sa_loc, sa_ctx)

    @pl.when(step % 2 == 0)
    def _():
        scores(qn_ref, kn_ref, kcn_ref, nxt, sb_loc, sb_ctx)
        finish(sa_loc, sa_ctx, v_ref, vc_ref, step, o_ref, p_loc_scr, p_ctx_scr, dn_scr)

    @pl.when(step % 2 == 1)
    def _():
        scores(qn_ref, kn_ref, kcn_ref, nxt, sa_loc, sa_ctx)
        finish(sb_loc, sb_ctx, v_ref, vc_ref, step, o_ref, p_loc_scr, p_ctx_scr, dn_scr)


def _next_step_maps(n_batch, n_steps):
    nb = lambda bi, i: jnp.minimum(bi + (i + 1) // n_steps, n_batch - 1)
    return (lambda bi, i: (nb(bi, i), (i + 1) % n_steps, 0)), (lambda bi, i: (nb(bi, i), 0, 0))


def _wg_call(sink, qb, kb, vb, kc, vc):
    b, s, _ = qb.shape
    c_len = kc.shape[1]
    n_blk = WG_BLOCKS_PER_STEP
    q_rows = n_blk * WG_BLOCK
    n_steps = s // q_rows
    assert n_steps % 2 == 0
    next_q, next_kv = _next_step_maps(b, n_steps)
    cur_q = lambda bi, i: (bi, i, 0)
    cur_kv = lambda bi, i: (bi, 0, 0)
    s_rows = n_blk * WG_HEADS * WG_BLOCK
    return pl.pallas_call(
        functools.partial(_wg_kernel, seq=s, n_blk=n_blk),
        grid=(b, n_steps),
        in_specs=[
            pl.BlockSpec(memory_space=pltpu.SMEM),
            pl.BlockSpec((None, q_rows, WG_W), cur_q),
            pl.BlockSpec((None, q_rows, WG_W), next_q),
            pl.BlockSpec((None, s, WG_KV_W), cur_kv),
            pl.BlockSpec((None, s, WG_KV_W), next_kv),
            pl.BlockSpec((None, s, WG_KV_W), cur_kv),
            pl.BlockSpec((None, c_len, WG_KV_W), cur_kv),
            pl.BlockSpec((None, c_len, WG_KV_W), next_kv),
            pl.BlockSpec((None, c_len, WG_KV_W), cur_kv),
        ],
        out_specs=pl.BlockSpec((None, q_rows, WG_W), cur_q),
        out_shape=jax.ShapeDtypeStruct((b, s, WG_W), _BF16),
        scratch_shapes=[pltpu.VMEM((s_rows, 3 * WG_BLOCK), _F32), pltpu.VMEM((s_rows, c_len), _F32),
                        pltpu.VMEM((s_rows, 3 * WG_BLOCK), _F32), pltpu.VMEM((s_rows, c_len), _F32),
                        pltpu.VMEM((s_rows, 3 * WG_BLOCK), _BF16), pltpu.VMEM((s_rows, c_len), _BF16),
                        pltpu.VMEM((s_rows, LANES), _F32)],
        compiler_params=_cparams(2),
        name="window_gqa",
    )(sink, qb, qb, kb, kb, vb, kc, kc, vc)


def _ctx_attn_kernel(sink_ref, qa_ref, ka_ref, va_ref, qb_ref, kb_ref, vb_ref, ya_ref, yb_ref):
    c_len = qa_ref.shape[0]
    low = _lane_id((c_len, LANES)) < HEAD_DIM
    for c in range(NA_HEADS // 2):
        cols = slice(c * LANES, (c + 1) * LANES)
        lhs = jnp.concatenate(_masked_halves(qa_ref[:, cols]), axis=0)
        s = _dot_nt(lhs, ka_ref[:, cols])
        (p,), denom = _softmax_parts([s])
        o = _dot(p, va_ref[:, cols]) / denom
        ya_ref[:, cols] = jnp.where(low, o[:c_len], o[c_len:]).astype(_BF16)
    kb = kb_ref[...]
    vb = vb_ref[...]
    for c in range(WG_HEADS // 2):
        cols = slice(c * LANES, (c + 1) * LANES)
        halves = _masked_halves(qb_ref[:, cols])
        outs = []
        for half in range(2):
            head = c + (WG_HEADS // 2) * half
            s = _dot_nt(halves[half], kb)
            (p,), denom = _softmax_parts([s], sink_ref[head])
            outs.append(_dot(p, vb) / denom)
        yb_ref[:, cols] = jnp.where(low, outs[0], outs[1]).astype(_BF16)


def _ctx_attn_call(sink, qa, ka, va, qb, kb, vb):
    b, c_len, _ = qa.shape
    spec = lambda w: pl.BlockSpec((None, c_len, w), lambda bi: (bi, 0, 0))
    return pl.pallas_call(
        _ctx_attn_kernel,
        grid=(b,),
        in_specs=[pl.BlockSpec(memory_space=pltpu.SMEM), spec(NA_W), spec(NA_W), spec(NA_W),
                  spec(WG_W), spec(WG_KV_W), spec(WG_KV_W)],
        out_specs=[spec(NA_W), spec(WG_W)],
        out_shape=[jax.ShapeDtypeStruct((b, c_len, NA_W), _BF16),
                   jax.ShapeDtypeStruct((b, c_len, WG_W), _BF16)],
        compiler_params=_cparams(1),
        name="context_attn",
    )(sink, qa, ka, va, qb, kb, vb)


HALO = 16


def _shift_rows(ext, lo, n):
    total = ext.shape[0]
    prev = pltpu.roll(ext, 1, 0)[lo:lo + n]
    nxt = pltpu.roll(ext, total - 1, 0)[lo:lo + n]
    return prev, nxt


def _tail_kernel(ya_ref, yap_ref, yan_ref, yb_ref, ybp_ref, ybn_ref, cu_ref, cup_ref, cun_ref,
                 bg_ref, bgp_ref, bgn_ref, x_ref, xp_ref, xn_ref, mod_ref, g_ref, convc_ref, wo_ref,
                 wu_ref, convf_ref, wd_ref, o_ref, xm_scr, h_scr, act_scr):
    i = pl.program_id(1)
    tile = x_ref.shape[0]
    ext_rows = tile + 2 * HALO
    prev_ok = (i > 0).astype(_F32)
    next_ok = (i < pl.num_programs(1) - 1).astype(_F32)
    cat = lambda p, m, n: jnp.concatenate([p, m, n], axis=0)

    cu = cat(cup_ref[...] * prev_ok, cu_ref[...], cun_ref[...] * next_ok)
    left, right = _shift_rows(cu, 0, ext_rows)
    yc = cat(bgp_ref[...], bg_ref[...], bgn_ref[...]) * (
        left * convc_ref[0:1, :] + cu * convc_ref[1:2, :] + right * convc_ref[2:3, :])
    ya = cat(yap_ref[...], ya_ref[...], yan_ref[...])
    yb = cat(ybp_ref[...], yb_ref[...], ybn_ref[...])
    yc = yc.astype(_BF16)
    x_ext = cat(xp_ref[...], x_ref[...], xn_ref[...])
    norm_gain = g_ref[...] * (1.0 + mod_ref[4:5, :])
    half = ext_rows // 2
    for hh in range(2):
        rows = slice(hh * half, (hh + 1) * half)
        y = (_dot(ya[rows], wo_ref[0:NA_W, :]) + _dot(yb[rows], wo_ref[NA_W:NA_W + WG_W, :])
             + _dot(yc[rows], wo_ref[NA_W + WG_W:, :]))
        xm = x_ext[rows] + mod_ref[2:3, :] * y
        xm_scr[rows, :] = xm
        row = hh * half + lax.broadcasted_iota(jnp.int32, (half, 1), 0)
        inside = jnp.where(row < HALO, prev_ok, jnp.where(row >= HALO + tile, next_ok, 1.0))
        h_scr[rows, :] = (_norm_mod(xm, norm_gain, mod_ref[3:4, :]) * inside).astype(_BF16)

    for c in range(N_FF_CHUNKS):
        a_cols = slice(c * FF_CHUNK, (c + 1) * FF_CHUNK)
        g_cols = slice(D_FF + c * FF_CHUNK, D_FF + (c + 1) * FF_CHUNK)
        h = h_scr[...]
        ua = _dot(h, wu_ref[:, a_cols])
        ug = _dot(h, wu_ref[:, g_cols])
        pa, na = _shift_rows(ua, HALO, tile)
        pg, ng = _shift_rows(ug, HALO, tile)
        mid = slice(HALO, HALO + tile)
        a = (pa * convf_ref[0:1, a_cols] + ua[mid] * convf_ref[1:2, a_cols]
             + na * convf_ref[2:3, a_cols])
        gt = (pg * convf_ref[0:1, g_cols] + ug[mid] * convf_ref[1:2, g_cols]
              + ng * convf_ref[2:3, g_cols])
        act_scr[:, a_cols] = ((a / (1.0 + jnp.exp(-a))) * gt).astype(_BF16)

    o_ref[...] = xm_scr[HALO:HALO + tile, :] + mod_ref[5:6, :] * _dot(act_scr[...], wd_ref[...])


def _tile_and_halo_specs(tile, width, n_rows):
    per = tile // HALO
    last = n_rows // HALO - 1
    main = pl.BlockSpec((None, tile, width), lambda bi, i: (bi, i, 0))
    prev = pl.BlockSpec((None, HALO, width), lambda bi, i: (bi, jnp.maximum(i * per - 1, 0), 0))
    nxt = pl.BlockSpec((None, HALO, width), lambda bi, i: (bi, jnp.minimum((i + 1) * per, last), 0))
    return [main, prev, nxt]


def _tail_call(ya, yb, cu, bg, x, mod, g, conv_c, w_o, wu, conv_f, wd, tile):
    b, s, _ = x.shape
    const = lambda shape: pl.BlockSpec(shape, lambda bi, i: (0,) * len(shape),
                                       pipeline_mode=pl.Buffered(1))
    in_specs = []
    for width in (NA_W, WG_W, SC_CH, SC_CH, D_MODEL):
        in_specs += _tile_and_halo_specs(tile, width, s)
    in_specs += [pl.BlockSpec((None, 6, D_MODEL), lambda bi, i: (bi, 0, 0)),
                 const((1, D_MODEL)), const(conv_c.shape), const(w_o.shape),
                 const(wu.shape), const(conv_f.shape), const(wd.shape)]
    return pl.pallas_call(
        _tail_kernel,
        grid=(b, s // tile),
        in_specs=in_specs,
        out_specs=pl.BlockSpec((None, tile, D_MODEL), lambda bi, i: (bi, i, 0)),
        out_shape=jax.ShapeDtypeStruct((b, s, D_MODEL), _F32),
        scratch_shapes=[pltpu.VMEM((tile + 2 * HALO, D_MODEL), _F32),
                        pltpu.VMEM((tile + 2 * HALO, D_MODEL), _BF16),
                        pltpu.VMEM((tile, D_FF), _BF16)],
        compiler_params=_cparams(2),
        name="layer_tail",
    )(ya, ya, ya, yb, yb, yb, cu, cu, cu, bg, bg, bg, x, x, x, mod, g, conv_c, w_o, wu, conv_f, wd)


def _wg_head_blocks(a, axis):
    order = []
    for c in range(WG_HEADS // 2):
        order += [c, c + WG_HEADS // 2]
    return [lax.slice_in_dim(a, h * HEAD_DIM, (h + 1) * HEAD_DIM, axis=axis) for h in order]


def _rope_tables(seq):
    t = np.arange(seq)
    row = (t // GRID_W).astype(np.float64)
    col = (t % GRID_W).astype(np.float64)
    half = HEAD_DIM // 2
    n_freq = half // 2
    inv = ROPE_BASE ** (-np.arange(n_freq, dtype=np.float64) / n_freq)
    ang = np.concatenate([row[:, None] * inv, col[:, None] * inv], axis=-1)
    cos = np.concatenate([np.cos(ang), np.cos(ang)], axis=-1)
    sin = np.concatenate([-np.sin(ang), np.sin(ang)], axis=-1)
    reps = LANES // HEAD_DIM
    return (jnp.asarray(np.tile(cos, (1, reps)), _F32), jnp.asarray(np.tile(sin, (1, reps)), _F32))


def _na_bias_table(rpb, n_rows):
    wr = min(NA_WIN_R, n_rows)
    q = np.arange(GRID_W)
    c0 = np.clip(q - NA_WIN_C // 2, 0, GRID_W - NA_WIN_C)
    kc = np.arange(GRID_W)
    ok = (kc[None, :] >= c0[:, None]) & (kc[None, :] < c0[:, None] + NA_WIN_C)
    pad = GRID_W - NA_WIN_C
    padded = jnp.pad(rpb.astype(_F32), ((0, 0), (0, 0), (pad, pad)))
    toep = jnp.stack([padded[:, :, GRID_W - 1 - qi:2 * GRID_W - 1 - qi] for qi in range(GRID_W)],
                     axis=1)
    toep = jnp.where(ok[None, :, None, :], toep * LOG2E, NEG)
    n_d = 2 * NA_WIN_R - 1
    toep = toep.reshape(NA_HEADS, GRID_W, n_d * GRID_W)
    return jnp.stack([toep[:, :, off * GRID_W:(off + wr) * GRID_W] for off in range(NA_WIN_R)],
                     axis=0)


def _pad_rows(a, rows):
    return jnp.concatenate([a, jnp.zeros((rows - a.shape[0],) + a.shape[1:], a.dtype)], axis=0)


def kernel(x, c, ctx, c_ctx, w_ada, b_ada, g_attn, w_in, qn_a, kn_a, qn_b, kn_b, rpb_a, sink_b,
           conv_c, w_o, g_ffn, w_up, conv_ffn, w_down):
    batch, seq, _ = x.shape
    c_len = ctx.shape[1]
    n_rows = seq // GRID_W
    scale = LOG2E / math.sqrt(HEAD_DIM)

    cvec = _pad_rows(jnp.concatenate([c, c_ctx[None, :]], axis=0), 2 * SUBLANES)
    mod = _ada_call(cvec, w_ada, b_ada)

    cos, sin = _rope_tables(seq)
    ones_c = jnp.ones((c_len, LANES), _F32)
    zeros_c = jnp.zeros((c_len, LANES), _F32)

    xl, xc = x, ctx
    for l in range(DEPTH):
        update_ctx = l < DEPTH - 1
        mod_l = mod[l, :batch].reshape(batch, 6, D_MODEL)
        mod_c = jnp.broadcast_to(mod[l, batch].reshape(1, 6, D_MODEL), (batch, 6, D_MODEL))

        w_in_l = w_in[l]
        w_in_l = jnp.concatenate(
            [w_in_l[:, :_QB]] + _wg_head_blocks(w_in_l[:, _QB:_KB], 1) + [w_in_l[:, _KB:]],
            axis=1).astype(_BF16)
        w_o_l = w_o[l]
        w_o_l = jnp.concatenate(
            [w_o_l[:NA_W]] + _wg_head_blocks(w_o_l[NA_W:NA_W + WG_W], 0) + [w_o_l[NA_W + WG_W:]],
            axis=0).astype(_BF16)
        tile4 = lambda v: jnp.tile(v, MXU_DIM // HEAD_DIM)
        gains = _pad_rows(jnp.stack([tile4(qn_a[l]) * scale, tile4(kn_a[l]),
                                     tile4(qn_b[l]) * scale, tile4(kn_b[l])]), SUBLANES)
        bias = _na_bias_table(rpb_a[l], n_rows)
        conv_c_l = _pad_rows(conv_c[l], SUBLANES)
        g_attn_l = g_attn[l].reshape(1, D_MODEL)
        g_ffn_l = g_ffn[l].reshape(1, D_MODEL)
        wu = w_up[l].astype(_BF16)
        wd = w_down[l].astype(_BF16)
        taps = _pad_rows(conv_ffn[l], SUBLANES)
        sink_perm = sink_b[l] * LOG2E

        qa, ka, va, qb, kb, vb, cu, bg = _inproj_call(
            xl, mod_l, g_attn_l, w_in_l, gains, cos, sin, INPROJ_TILE)
        qa_c, ka_c, va_c, qb_c, kb_c, vb_c, cu_c, bg_c = _inproj_call(
            xc, mod_c, g_attn_l, w_in_l, gains, ones_c, zeros_c, c_len)
        ya = _na_call(qa, ka, va, ka_c, va_c, bias)
        yb = _wg_call(sink_perm, qb, kb, vb, kb_c, vb_c)
        xl = _tail_call(ya, yb, cu, bg, xl, mod_l, g_ffn_l, conv_c_l, w_o_l, wu, taps, wd,
                        TOKEN_TILE)
        if update_ctx:
            ya_c, yb_c = _ctx_attn_call(sink_perm, qa_c, ka_c, va_c, qb_c, kb_c, vb_c)
            xc = _tail_call(ya_c, yb_c, cu_c, bg_c, xc, mod_c, g_ffn_l, conv_c_l, w_o_l, wu, taps,
                            wd, c_len)
    return xl
```

```python
import functools
import math

import jax
import jax.numpy as jnp
import numpy as np
from jax import lax
from jax.experimental import pallas as pl
from jax.experimental.pallas import tpu as pltpu

D_MODEL = 1024
DEPTH = 2
GRID_W = 64
HEAD_DIM = 64
NA_HEADS = 6
NA_WIN_R = 8
NA_WIN_C = 16
WG_HEADS = 6
WG_KV_HEADS = 2
WG_WINDOW = 128
WG_BLOCK = 128
SC_CH = 256
CONV_W = 3
D_FF = 2816
ROPE_BASE = 10000.0
EPS = 1e-6
NEG = -1e30
LOG2E = math.log2(math.e)

NA_W = NA_HEADS * HEAD_DIM
WG_W = WG_HEADS * HEAD_DIM
WG_KV_W = WG_KV_HEADS * HEAD_DIM
IN_W = 3 * NA_W + WG_W + 2 * WG_KV_W + 3 * SC_CH

LANES = 128
SUBLANES = 8
MXU_DIM = 256
FF_CHUNK = MXU_DIM
N_FF_CHUNKS = D_FF // FF_CHUNK
TOKEN_TILE = 512
INPROJ_TILE = 1024
INPROJ_SUB_TILE = 256
NA_ROWS_PER_STEP = 16
SOFTMAX_ROWS = 32
WG_BLOCKS_PER_STEP = 4
VMEM_LIMIT = 48 * 1024 * 1024

_QA, _KA, _VA = 0, NA_W, 2 * NA_W
_QB = 3 * NA_W
_KB = _QB + WG_W
_VB = _KB + WG_KV_W
_U = _VB + WG_KV_W
_BG = _U + SC_CH
_CG = _BG + SC_CH

_F32 = jnp.float32
_BF16 = jnp.bfloat16


def _dot(a, b):
    return jnp.dot(a, b, preferred_element_type=_F32)


def _dot_nt(a, b):
    return lax.dot_general(a, b, (((1,), (1,)), ((), ())), preferred_element_type=_F32)


def _cparams(n_grid, fuse_inputs=None):
    return pltpu.CompilerParams(
        dimension_semantics=("arbitrary",) * n_grid, vmem_limit_bytes=VMEM_LIMIT,
        allow_input_fusion=fuse_inputs)


def _ada_kernel(c_ref, w_ref, b_ref, o_ref):
    c = c_ref[...]
    s = c / (1.0 + jnp.exp(-c))
    s_hi = s.astype(_BF16)
    s_lo = (s - s_hi.astype(_F32)).astype(_BF16)
    w = w_ref[...]
    w_hi = w.astype(_BF16)
    w_lo = (w - w_hi.astype(_F32)).astype(_BF16)
    o_ref[...] = _dot(s_hi, w_hi) + _dot(s_lo, w_hi) + _dot(s_hi, w_lo) + b_ref[...]


def _ada_call(cvec, w_ada, b_ada):
    rows = cvec.shape[0]
    n_out = w_ada.shape[-1]
    nt = 1536
    return pl.pallas_call(
        _ada_kernel,
        grid=(DEPTH, n_out // nt),
        in_specs=[
            pl.BlockSpec((rows, D_MODEL), lambda l, j: (0, 0)),
            pl.BlockSpec((None, D_MODEL, nt), lambda l, j: (l, 0, j)),
            pl.BlockSpec((None, 1, nt), lambda l, j: (l, 0, j)),
        ],
        out_specs=pl.BlockSpec((None, rows, nt), lambda l, j: (l, 0, j)),
        out_shape=jax.ShapeDtypeStruct((DEPTH, rows, n_out), _F32),
        compiler_params=_cparams(2),
        name="adaln_mod",
    )(cvec, w_ada, b_ada.reshape(DEPTH, 1, n_out))


def _norm_mod(x, gain, shift):
    ms = jnp.mean(x * x, axis=-1, keepdims=True)
    return x * lax.rsqrt(ms + EPS) * gain + shift


def _lane_id(shape):
    return lax.broadcasted_iota(jnp.int32, shape, len(shape) - 1)


def _head_norm(x, gain):
    outs = []
    for j in range(x.shape[1] // LANES):
        xc = x[:, j * LANES:(j + 1) * LANES]
        sq = xc * xc
        low = _lane_id(xc.shape) < HEAD_DIM
        s_lo = jnp.sum(jnp.where(low, sq, 0.0), axis=-1, keepdims=True)
        s_hi = jnp.sum(jnp.where(low, 0.0, sq), axis=-1, keepdims=True)
        ms = jnp.where(low, s_lo, s_hi) * (1.0 / HEAD_DIM)
        outs.append(xc * lax.rsqrt(ms + EPS))
    return jnp.concatenate(outs, axis=1) * gain


def _inproj_kernel(x_ref, mod_ref, g_ref, w_ref, gain_ref, cos_ref, sin_ref,
                   qa_ref, ka_ref, va_ref, qb_ref, kb_ref, vb_ref, cu_ref, bg_ref, p_scr, *, sub):
    gain_qk_a = jnp.concatenate([gain_ref[0:1, :LANES], gain_ref[1:2, :LANES]], axis=1)
    gain_qk_b = jnp.concatenate([gain_ref[2:3, :LANES], gain_ref[3:4, :LANES]], axis=1)
    first_half = (_lane_id((sub, LANES)) % HEAD_DIM) < (HEAD_DIM // 2)
    norm_gain = g_ref[...] * (1.0 + mod_ref[1:2, :])
    for t in range(x_ref.shape[0] // sub):
        rows = slice(t * sub, (t + 1) * sub)
        h = _norm_mod(x_ref[rows, :], norm_gain, mod_ref[0:1, :]).astype(_BF16)
        p_scr[rows, :] = _dot(h, w_ref[...])

        p = p_scr[rows, _QA:_QA + MXU_DIM]
        qa_ref[rows, 0:MXU_DIM] = _head_norm(p,gain_ref[0:1, :]).astype(_BF16)
        p = p_scr[rows, _QA + MXU_DIM:_QA + 2 * MXU_DIM]
        pn = _head_norm(p,gain_qk_a).astype(_BF16)
        qa_ref[rows, MXU_DIM:NA_W] = pn[:, :LANES]
        ka_ref[rows, 0:LANES] = pn[:, LANES:]
        p = p_scr[rows, _QA + 2 * MXU_DIM:_QA + 3 * MXU_DIM]
        ka_ref[rows, LANES:NA_W] = _head_norm(p,gain_ref[1:2, :]).astype(_BF16)

        va_ref[rows, :] = p_scr[rows, _VA:_VA + NA_W].astype(_BF16)

        cos = cos_ref[rows, :]
        sin = sin_ref[rows, :]
        for k in range(2):
            p = p_scr[rows, _QB + k * MXU_DIM:_QB + (k + 1) * MXU_DIM]
            pn = _head_norm(p,gain_ref[2:3, :] if k == 0 else gain_qk_b)
            for j in range(2):
                xc = pn[:, j * LANES:(j + 1) * LANES]
                swapped = jnp.where(first_half, pltpu.roll(xc, LANES - HEAD_DIM // 2, 1),
                                    pltpu.roll(xc, HEAD_DIM // 2, 1))
                xr = (xc * cos + swapped * sin).astype(_BF16)
                col = k * MXU_DIM + j * LANES
                if col < WG_W:
                    qb_ref[rows, col:col + LANES] = xr
                else:
                    kb_ref[rows, :] = xr

        vb_ref[rows, :] = p_scr[rows, _VB:_VB + WG_KV_W].astype(_BF16)
        bg_ref[rows, :] = p_scr[rows, _BG:_BG + SC_CH]
        cu_ref[rows, :] = p_scr[rows, _CG:_CG + SC_CH] * p_scr[rows, _U:_U + SC_CH]


def _inproj_call(x, mod, g, w_in, gains, cos, sin, tile):
    b, s, _ = x.shape
    nt = s // tile
    tok = lambda width: pl.BlockSpec((None, tile, width), lambda bi, i: (bi, i, 0))
    const = lambda shape: pl.BlockSpec(shape, lambda bi, i: (0,) * len(shape),
                                       pipeline_mode=pl.Buffered(1))
    widths = (NA_W, NA_W, NA_W, WG_W, WG_KV_W, WG_KV_W, SC_CH, SC_CH)
    dtypes = (_BF16,) * 6 + (_F32, _F32)
    return pl.pallas_call(
        functools.partial(_inproj_kernel, sub=min(tile, INPROJ_SUB_TILE)),
        grid=(b, nt),
        in_specs=[
            tok(D_MODEL),
            pl.BlockSpec((None, 6, D_MODEL), lambda bi, i: (bi, 0, 0)),
            const((1, D_MODEL)),
            const((D_MODEL, IN_W)),
            const((SUBLANES, MXU_DIM)),
            pl.BlockSpec((tile, LANES), lambda bi, i: (i, 0)),
            pl.BlockSpec((tile, LANES), lambda bi, i: (i, 0)),
        ],
        out_specs=[tok(w) for w in widths],
        out_shape=[jax.ShapeDtypeStruct((b, s, w), d) for w, d in zip(widths, dtypes)],
        scratch_shapes=[pltpu.VMEM((tile, IN_W), _F32)],
        compiler_params=_cparams(2, [False] * 3 + [True] + [False] * 3),
        name="mixer_inproj",
    )(x, mod, g, w_in, gains, cos, sin)


def _masked_halves(qp):
    qf = qp.astype(_F32)
    low = _lane_id(qf.shape) < HEAD_DIM
    return (jnp.where(low, qf, 0.0).astype(_BF16), jnp.where(low, 0.0, qf).astype(_BF16))


def _softmax_parts(parts, extra=None):
    m = parts[0].max(axis=-1, keepdims=True)
    for s in parts[1:]:
        m = jnp.maximum(m, s.max(axis=-1, keepdims=True))
    if extra is not None:
        m = jnp.maximum(m, extra)
    ps = [jnp.exp2(s - m) for s in parts]
    denom = ps[0].sum(axis=-1, keepdims=True)
    for p in ps[1:]:
        denom = denom + p.sum(axis=-1, keepdims=True)
    if extra is not None:
        denom = denom + jnp.exp2(extra - m)
    return [p.astype(_BF16) for p in ps], denom


def _na_kernel(q_ref, k_ref, v_ref, kc_ref, vc_ref, bias_ref, o_ref, *, rows_per_step, n_rows):
    step = pl.program_id(1)
    wr = min(NA_WIN_R, n_rows)
    low = _lane_id((GRID_W, LANES)) < HEAD_DIM
    offs, kstarts = [], []
    for rr in range(rows_per_step):
        r = step * rows_per_step + rr
        r0 = jnp.clip(r - wr // 2, 0, n_rows - wr)
        offs.append(r0 - r + NA_WIN_R - 1)
        kstarts.append(pl.multiple_of(r0 * GRID_W, GRID_W))

    for c in range(NA_HEADS // 2):
        cols = slice(c * LANES, (c + 1) * LANES)
        lhs = jnp.concatenate(
            [half for rr in range(rows_per_step)
             for half in _masked_halves(q_ref[rr * GRID_W:(rr + 1) * GRID_W, cols])], axis=0)
        s_ctx = _dot_nt(lhs, kc_ref[:, cols])
        p_ctx, o_loc, denoms = [], [], []
        for rr in range(rows_per_step):
            rows = slice(rr * 2 * GRID_W, (rr + 1) * 2 * GRID_W)
            kl = k_ref[pl.ds(kstarts[rr], wr * GRID_W), cols]
            vl = v_ref[pl.ds(kstarts[rr], wr * GRID_W), cols]
            bias = jnp.concatenate(
                [bias_ref[offs[rr], 2 * c], bias_ref[offs[rr], 2 * c + 1]], axis=0)
            s_loc = _dot_nt(lhs[rows], kl) + bias
            (pl_, pc_), dn = _softmax_parts([s_loc, s_ctx[rows]])
            o_loc.append(_dot(pl_, vl))
            p_ctx.append(pc_)
            denoms.append(dn)
        o_ctx = _dot(jnp.concatenate(p_ctx, axis=0), vc_ref[:, cols])
        for rr in range(rows_per_step):
            rows = slice(rr * 2 * GRID_W, (rr + 1) * 2 * GRID_W)
            o = (o_loc[rr] + o_ctx[rows]) / denoms[rr]
            o_ref[rr * GRID_W:(rr + 1) * GRID_W, cols] = jnp.where(
                low, o[:GRID_W], o[GRID_W:]).astype(_BF16)


def _na_call(qa, ka, va, kc, vc, bias):
    b, s, _ = qa.shape
    n_rows = s // GRID_W
    c_len = kc.shape[1]
    rps = NA_ROWS_PER_STEP
    blk = rps * GRID_W
    return pl.pallas_call(
        functools.partial(_na_kernel, rows_per_step=rps, n_rows=n_rows),
        grid=(b, n_rows // rps),
        in_specs=[
            pl.BlockSpec((None, blk, NA_W), lambda bi, j: (bi, j, 0)),
            pl.BlockSpec((None, s, NA_W), lambda bi, j: (bi, 0, 0)),
            pl.BlockSpec((None, s, NA_W), lambda bi, j: (bi, 0, 0)),
            pl.BlockSpec((None, c_len, NA_W), lambda bi, j: (bi, 0, 0)),
            pl.BlockSpec((None, c_len, NA_W), lambda bi, j: (bi, 0, 0)),
            pl.BlockSpec(bias.shape, lambda bi, j: (0, 0, 0, 0)),
        ],
        out_specs=pl.BlockSpec((None, blk, NA_W), lambda bi, j: (bi, j, 0)),
        out_shape=jax.ShapeDtypeStruct((b, s, NA_W), _BF16),
        compiler_params=_cparams(2),
        name="neighbourhood_attn",
    )(qa, ka, va, kc, vc, bias)


def _wg_window_start(blk, seq):
    span = 3 * WG_BLOCK
    return pl.multiple_of(jnp.clip((blk - 1) * WG_BLOCK, 0, seq - span), WG_BLOCK)


def _wg_scores(q_ref, k_ref, kc_ref, step, s_loc_scr, s_ctx_scr, *, seq, n_blk):
    span = 3 * WG_BLOCK
    per_blk = WG_HEADS * WG_BLOCK
    pieces = []
    for bb in range(n_blk):
        for c in range(WG_HEADS // 2):
            pieces.extend(_masked_halves(
                q_ref[bb * WG_BLOCK:(bb + 1) * WG_BLOCK, c * LANES:(c + 1) * LANES]))
    lhs = jnp.concatenate(pieces, axis=0)
    s_ctx_scr[...] = _dot_nt(lhs, kc_ref[...])
    for bb in range(n_blk):
        start = _wg_window_start(step * n_blk + bb, seq)
        s_loc_scr[bb * per_blk:(bb + 1) * per_blk, :] = _dot_nt(
            lhs[bb * per_blk:(bb + 1) * per_blk], k_ref[pl.ds(start, span), :])


def _wg_softmax_pv(sink_ref, s_loc_scr, s_ctx_scr, v_ref, vc_ref, step, o_ref,
                   p_loc_scr, p_ctx_scr, dn_scr, *, seq, n_blk):
    span = 3 * WG_BLOCK
    per_blk = WG_HEADS * WG_BLOCK
    low = _lane_id((WG_BLOCK, LANES)) < HEAD_DIM
    row_id = lax.broadcasted_iota(jnp.int32, (SOFTMAX_ROWS, span), 0)
    col_id = lax.broadcasted_iota(jnp.int32, (SOFTMAX_ROWS, span), 1)
    o_loc = []
    for bb in range(n_blk):
        blk = step * n_blk + bb
        start = _wg_window_start(blk, seq)
        for rc in range(WG_BLOCK // SOFTMAX_ROWS):
            dist = (blk * WG_BLOCK - start + rc * SOFTMAX_ROWS) + row_id - col_id
            mask_add = jnp.where(jnp.abs(dist) <= WG_WINDOW, 0.0, NEG).astype(_F32)
            for hb in range(WG_HEADS):
                head = (hb // 2) + (WG_HEADS // 2) * (hb % 2)
                r0 = bb * per_blk + hb * WG_BLOCK + rc * SOFTMAX_ROWS
                rows = slice(r0, r0 + SOFTMAX_ROWS)
                (pl_, pc_), dn = _softmax_parts(
                    [s_loc_scr[rows, :] + mask_add, s_ctx_scr[rows, :]], sink_ref[head])
                p_loc_scr[rows, :] = pl_
                p_ctx_scr[rows, :] = pc_
                dn_scr[rows, :] = jnp.broadcast_to(dn, (SOFTMAX_ROWS, LANES))
        o_loc.append(_dot(p_loc_scr[bb * per_blk:(bb + 1) * per_blk, :],
                          v_ref[pl.ds(start, span), :]))
    o_ctx = _dot(p_ctx_scr[...], vc_ref[...])
    for bb in range(n_blk):
        for c in range(WG_HEADS // 2):
            halves = []
            for half in range(2):
                hb = 2 * c + half
                rows = slice(hb * WG_BLOCK, (hb + 1) * WG_BLOCK)
                rows_all = slice(bb * per_blk + hb * WG_BLOCK, bb * per_blk + (hb + 1) * WG_BLOCK)
                halves.append((o_loc[bb][rows] + o_ctx[rows_all]) / dn_scr[rows_all, :])
            o_ref[bb * WG_BLOCK:(bb + 1) * WG_BLOCK, c * LANES:(c + 1) * LANES] = jnp.where(
                low, halves[0], halves[1]).astype(_BF16)


def _wg_kernel(sink_ref, q_ref, qn_ref, k_ref, kn_ref, v_ref, kc_ref, kcn_ref, vc_ref, o_ref,
               sa_loc, sa_ctx, sb_loc, sb_ctx, p_loc_scr, p_ctx_scr, dn_scr, *, seq, n_blk):
    bi, step = pl.program_id(0), pl.program_id(1)
    nxt = (step + 1) % pl.num_programs(1)
    scores = functools.partial(_wg_scores, seq=seq, n_blk=n_blk)
    finish = functools.partial(_wg_softmax_pv, sink_ref, seq=seq, n_blk=n_blk)

    @pl.when((bi == 0) & (step == 0))
    def _():
        scores(q_ref, k_ref, kc_ref, step, sa_loc, sa_ctx)

    @pl.when(step % 2 == 0)
    def _():
        scores(qn_ref, kn_ref, kcn_ref, nxt, sb_loc, sb_ctx)
        finish(sa_loc, sa_ctx, v_ref, vc_ref, step, o_ref, p_loc_scr, p_ctx_scr, dn_scr)

    @pl.when(step % 2 == 1)
    def _():
        scores(qn_ref, kn_ref, kcn_ref, nxt, sa_loc, sa_ctx)
        finish(sb_loc, sb_ctx, v_ref, vc_ref, step, o_ref, p_loc_scr, p_ctx_scr, dn_scr)


def _next_step_maps(n_batch, n_steps):
    nb = lambda bi, i: jnp.minimum(bi + (i + 1) // n_steps, n_batch - 1)
    return (lambda bi, i: (nb(bi, i), (i + 1) % n_steps, 0)), (lambda bi, i: (nb(bi, i), 0, 0))


def _wg_call(sink, qb, kb, vb, kc, vc):
    b, s, _ = qb.shape
    c_len = kc.shape[1]
    n_blk = WG_BLOCKS_PER_STEP
    q_rows = n_blk * WG_BLOCK
    n_steps = s // q_rows
    assert n_steps % 2 == 0
    next_q, next_kv = _next_step_maps(b, n_steps)
    cur_q = lambda bi, i: (bi, i, 0)
    cur_kv = lambda bi, i: (bi, 0, 0)
    s_rows = n_blk * WG_HEADS * WG_BLOCK
    return pl.pallas_call(
        functools.partial(_wg_kernel, seq=s, n_blk=n_blk),
        grid=(b, n_steps),
        in_specs=[
            pl.BlockSpec(memory_space=pltpu.SMEM),
            pl.BlockSpec((None, q_rows, WG_W), cur_q),
            pl.BlockSpec((None, q_rows, WG_W), next_q),
            pl.BlockSpec((None, s, WG_KV_W), cur_kv),
            pl.BlockSpec((None, s, WG_KV_W), next_kv),
            pl.BlockSpec((None, s, WG_KV_W), cur_kv),
            pl.BlockSpec((None, c_len, WG_KV_W), cur_kv),
            pl.BlockSpec((None, c_len, WG_KV_W), next_kv),
            pl.BlockSpec((None, c_len, WG_KV_W), cur_kv),
        ],
        out_specs=pl.BlockSpec((None, q_rows, WG_W), cur_q),
        out_shape=jax.ShapeDtypeStruct((b, s, WG_W), _BF16),
        scratch_shapes=[pltpu.VMEM((s_rows, 3 * WG_BLOCK), _F32), pltpu.VMEM((s_rows, c_len), _F32),
                        pltpu.VMEM((s_rows, 3 * WG_BLOCK), _F32), pltpu.VMEM((s_rows, c_len), _F32),
                        pltpu.VMEM((s_rows, 3 * WG_BLOCK), _BF16), pltpu.VMEM((s_rows, c_len), _BF16),
                        pltpu.VMEM((s_rows, LANES), _F32)],
        compiler_params=_cparams(2),
        name="window_gqa",
    )(sink, qb, qb, kb, kb, vb, kc, kc, vc)


def _ctx_attn_kernel(sink_ref, qa_ref, ka_ref, va_ref, qb_ref, kb_ref, vb_ref, ya_ref, yb_ref):
    c_len = qa_ref.shape[0]
    low = _lane_id((c_len, LANES)) < HEAD_DIM
    for c in range(NA_HEADS // 2):
        cols = slice(c * LANES, (c + 1) * LANES)
        lhs = jnp.concatenate(_masked_halves(qa_ref[:, cols]), axis=0)
        s = _dot_nt(lhs, ka_ref[:, cols])
        (p,), denom = _softmax_parts([s])
        o = _dot(p, va_ref[:, cols]) / denom
        ya_ref[:, cols] = jnp.where(low, o[:c_len], o[c_len:]).astype(_BF16)
    kb = kb_ref[...]
    vb = vb_ref[...]
    for c in range(WG_HEADS // 2):
        cols = slice(c * LANES, (c + 1) * LANES)
        halves = _masked_halves(qb_ref[:, cols])
        outs = []
        for half in range(2):
            head = c + (WG_HEADS // 2) * half
            s = _dot_nt(halves[half], kb)
            (p,), denom = _softmax_parts([s], sink_ref[head])
            outs.append(_dot(p, vb) / denom)
        yb_ref[:, cols] = jnp.where(low, outs[0], outs[1]).astype(_BF16)


def _ctx_attn_call(sink, qa, ka, va, qb, kb, vb):
    b, c_len, _ = qa.shape
    spec = lambda w: pl.BlockSpec((None, c_len, w), lambda bi: (bi, 0, 0))
    return pl.pallas_call(
        _ctx_attn_kernel,
        grid=(b,),
        in_specs=[pl.BlockSpec(memory_space=pltpu.SMEM), spec(NA_W), spec(NA_W), spec(NA_W),
                  spec(WG_W), spec(WG_KV_W), spec(WG_KV_W)],
        out_specs=[spec(NA_W), spec(WG_W)],
        out_shape=[jax.ShapeDtypeStruct((b, c_len, NA_W), _BF16),
                   jax.ShapeDtypeStruct((b, c_len, WG_W), _BF16)],
        compiler_params=_cparams(1),
        name="context_attn",
    )(sink, qa, ka, va, qb, kb, vb)


HALO = 16


def _shift_rows(ext, lo, n):
    total = ext.shape[0]
    prev = pltpu.roll(ext, 1, 0)[lo:lo + n]
    nxt = pltpu.roll(ext, total - 1, 0)[lo:lo + n]
    return prev, nxt


def _tail_kernel(ya_ref, yap_ref, yan_ref, yb_ref, ybp_ref, ybn_ref, cu_ref, cup_ref, cun_ref,
                 bg_ref, bgp_ref, bgn_ref, x_ref, xp_ref, xn_ref, mod_ref, g_ref, convc_ref, wo_ref,
                 wu_ref, convf_ref, wd_ref, o_ref, xm_scr, h_scr, act_scr):
    i = pl.program_id(1)
    tile = x_ref.shape[0]
    ext_rows = tile + 2 * HALO
    prev_ok = (i > 0).astype(_F32)
    next_ok = (i < pl.num_programs(1) - 1).astype(_F32)
    cat = lambda p, m, n: jnp.concatenate([p, m, n], axis=0)

    cu = cat(cup_ref[...] * prev_ok, cu_ref[...], cun_ref[...] * next_ok)
    left, right = _shift_rows(cu, 0, ext_rows)
    yc = cat(bgp_ref[...], bg_ref[...], bgn_ref[...]) * (
        left * convc_ref[0:1, :] + cu * convc_ref[1:2, :] + right * convc_ref[2:3, :])
    ya = cat(yap_ref[...], ya_ref[...], yan_ref[...])
    yb = cat(ybp_ref[...], yb_ref[...], ybn_ref[...])
    yc = yc.astype(_BF16)
    x_ext = cat(xp_ref[...], x_ref[...], xn_ref[...])
    norm_gain = g_ref[...] * (1.0 + mod_ref[4:5, :])
    half = ext_rows // 2
    for hh in range(2):
        rows = slice(hh * half, (hh + 1) * half)
        y = (_dot(ya[rows], wo_ref[0:NA_W, :]) + _dot(yb[rows], wo_ref[NA_W:NA_W + WG_W, :])
             + _dot(yc[rows], wo_ref[NA_W + WG_W:, :]))
        xm = x_ext[rows] + mod_ref[2:3, :] * y
        xm_scr[rows, :] = xm
        row = hh * half + lax.broadcasted_iota(jnp.int32, (half, 1), 0)
        inside = jnp.where(row < HALO, prev_ok, jnp.where(row >= HALO + tile, next_ok, 1.0))
        h_scr[rows, :] = (_norm_mod(xm, norm_gain, mod_ref[3:4, :]) * inside).astype(_BF16)

    for c in range(N_FF_CHUNKS):
        a_cols = slice(c * FF_CHUNK, (c + 1) * FF_CHUNK)
        g_cols = slice(D_FF + c * FF_CHUNK, D_FF + (c + 1) * FF_CHUNK)
        h = h_scr[...]
        ua = _dot(h, wu_ref[:, a_cols])
        ug = _dot(h, wu_ref[:, g_cols])
        pa, na = _shift_rows(ua, HALO, tile)
        pg, ng = _shift_rows(ug, HALO, tile)
        mid = slice(HALO, HALO + tile)
        a = (pa * convf_ref[0:1, a_cols] + ua[mid] * convf_ref[1:2, a_cols]
             + na * convf_ref[2:3, a_cols])
        gt = (pg * convf_ref[0:1, g_cols] + ug[mid] * convf_ref[1:2, g_cols]
              + ng * convf_ref[2:3, g_cols])
        act_scr[:, a_cols] = ((a / (1.0 + jnp.exp(-a))) * gt).astype(_BF16)

    o_ref[...] = xm_scr[HALO:HALO + tile, :] + mod_ref[5:6, :] * _dot(act_scr[...], wd_ref[...])


def _tile_and_halo_specs(tile, width, n_rows):
    per = tile // HALO
    last = n_rows // HALO - 1
    main = pl.BlockSpec((None, tile, width), lambda bi, i: (bi, i, 0))
    prev = pl.BlockSpec((None, HALO, width), lambda bi, i: (bi, jnp.maximum(i * per - 1, 0), 0))
    nxt = pl.BlockSpec((None, HALO, width), lambda bi, i: (bi, jnp.minimum((i + 1) * per, last), 0))
    return [main, prev, nxt]


def _tail_call(ya, yb, cu, bg, x, mod, g, conv_c, w_o, wu, conv_f, wd, tile):
    b, s, _ = x.shape
    const = lambda shape: pl.BlockSpec(shape, lambda bi, i: (0,) * len(shape),
                                       pipeline_mode=pl.Buffered(1))
    in_specs = []
    for width in (NA_W, WG_W, SC_CH, SC_CH, D_MODEL):
        in_specs += _tile_and_halo_specs(tile, width, s)
    in_specs += [pl.BlockSpec((None, 6, D_MODEL), lambda bi, i: (bi, 0, 0)),
                 const((1, D_MODEL)), const(conv_c.shape), const(w_o.shape),
                 const(wu.shape), const(conv_f.shape), const(wd.shape)]
    return pl.pallas_call(
        _tail_kernel,
        grid=(b, s // tile),
        in_specs=in_specs,
        out_specs=pl.BlockSpec((None, tile, D_MODEL), lambda bi, i: (bi, i, 0)),
        out_shape=jax.ShapeDtypeStruct((b, s, D_MODEL), _F32),
        scratch_shapes=[pltpu.VMEM((tile + 2 * HALO, D_MODEL), _F32),
                        pltpu.VMEM((tile + 2 * HALO, D_MODEL), _BF16),
                        pltpu.VMEM((tile, D_FF), _BF16)],
        compiler_params=_cparams(2, [False] * 18 + [True, True, False, True]),
        name="layer_tail",
    )(ya, ya, ya, yb, yb, yb, cu, cu, cu, bg, bg, bg, x, x, x, mod, g, conv_c, w_o, wu, conv_f, wd)


def _wg_head_blocks(a, axis):
    order = []
    for c in range(WG_HEADS // 2):
        order += [c, c + WG_HEADS // 2]
    return [lax.slice_in_dim(a, h * HEAD_DIM, (h + 1) * HEAD_DIM, axis=axis) for h in order]


def _rope_tables(seq):
    t = np.arange(seq)
    row = (t // GRID_W).astype(np.float64)
    col = (t % GRID_W).astype(np.float64)
    half = HEAD_DIM // 2
    n_freq = half // 2
    inv = ROPE_BASE ** (-np.arange(n_freq, dtype=np.float64) / n_freq)
    ang = np.concatenate([row[:, None] * inv, col[:, None] * inv], axis=-1)
    cos = np.concatenate([np.cos(ang), np.cos(ang)], axis=-1)
    sin = np.concatenate([-np.sin(ang), np.sin(ang)], axis=-1)
    reps = LANES // HEAD_DIM
    return (jnp.asarray(np.tile(cos, (1, reps)), _F32), jnp.asarray(np.tile(sin, (1, reps)), _F32))


def _na_bias_table(rpb, n_rows):
    wr = min(NA_WIN_R, n_rows)
    q = np.arange(GRID_W)
    c0 = np.clip(q - NA_WIN_C // 2, 0, GRID_W - NA_WIN_C)
    kc = np.arange(GRID_W)
    ok = (kc[None, :] >= c0[:, None]) & (kc[None, :] < c0[:, None] + NA_WIN_C)
    pad = GRID_W - NA_WIN_C
    padded = jnp.pad(rpb.astype(_F32), ((0, 0), (0, 0), (pad, pad)))
    toep = jnp.stack([padded[:, :, GRID_W - 1 - qi:2 * GRID_W - 1 - qi] for qi in range(GRID_W)],
                     axis=1)
    toep = jnp.where(ok[None, :, None, :], toep * LOG2E, NEG)
    n_d = 2 * NA_WIN_R - 1
    toep = toep.reshape(NA_HEADS, GRID_W, n_d * GRID_W)
    return jnp.stack([toep[:, :, off * GRID_W:(off + wr) * GRID_W] for off in range(NA_WIN_R)],
                     axis=0)


def _pad_rows(a, rows):
    return jnp.concatenate([a, jnp.zeros((rows - a.shape[0],) + a.shape[1:], a.dtype)], axis=0)


def kernel(x, c, ctx, c_ctx, w_ada, b_ada, g_attn, w_in, qn_a, kn_a, qn_b, kn_b, rpb_a, sink_b,
           conv_c, w_o, g_ffn, w_up, conv_ffn, w_down):
    batch, seq, _ = x.shape
    c_len = ctx.shape[1]
    n_rows = seq // GRID_W
    scale = LOG2E / math.sqrt(HEAD_DIM)

    cvec = _pad_rows(jnp.concatenate([c, c_ctx[None, :]], axis=0), 2 * SUBLANES)
    mod = _ada_call(cvec, w_ada, b_ada)

    cos, sin = _rope_tables(seq)
    ones_c = jnp.ones((c_len, LANES), _F32)
    zeros_c = jnp.zeros((c_len, LANES), _F32)

    xl, xc = x, ctx
    for l in range(DEPTH):
        update_ctx = l < DEPTH - 1
        mod_l = mod[l, :batch].reshape(batch, 6, D_MODEL)
        mod_c = jnp.broadcast_to(mod[l, batch].reshape(1, 6, D_MODEL), (batch, 6, D_MODEL))

        w_in_l = w_in[l]
        w_in_l = jnp.concatenate(
            [w_in_l[:, :_QB]] + _wg_head_blocks(w_in_l[:, _QB:_KB], 1) + [w_in_l[:, _KB:]],
            axis=1).astype(_BF16)
        w_o_l = w_o[l]
        w_o_l = jnp.concatenate(
            [w_o_l[:NA_W]] + _wg_head_blocks(w_o_l[NA_W:NA_W + WG_W], 0) + [w_o_l[NA_W + WG_W:]],
            axis=0).astype(_BF16)
        tile4 = lambda v: jnp.tile(v, MXU_DIM // HEAD_DIM)
        gains = _pad_rows(jnp.stack([tile4(qn_a[l]) * scale, tile4(kn_a[l]),
                                     tile4(qn_b[l]) * scale, tile4(kn_b[l])]), SUBLANES)
        bias = _na_bias_table(rpb_a[l], n_rows)
        conv_c_l = _pad_rows(conv_c[l], SUBLANES)
        g_attn_l = g_attn[l].reshape(1, D_MODEL)
        g_ffn_l = g_ffn[l].reshape(1, D_MODEL)
        wu = w_up[l].astype(_BF16)
        wd = w_down[l].astype(_BF16)
        taps = _pad_rows(conv_ffn[l], SUBLANES)
        sink_perm = sink_b[l] * LOG2E

        qa, ka, va, qb, kb, vb, cu, bg = _inproj_call(
            xl, mod_l, g_attn_l, w_in_l, gains, cos, sin, INPROJ_TILE)
        qa_c, ka_c, va_c, qb_c, kb_c, vb_c, cu_c, bg_c = _inproj_call(
            xc, mod_c, g_attn_l, w_in_l, gains, ones_c, zeros_c, c_len)
        ya = _na_call(qa, ka, va, ka_c, va_c, bias)
        yb = _wg_call(sink_perm, qb, kb, vb, kb_c, vb_c)
        xl = _tail_call(ya, yb, cu, bg, xl, mod_l, g_ffn_l, conv_c_l, w_o_l, wu, taps, wd,
                        TOKEN_TILE)
        if update_ctx:
            ya_c, yb_c = _ctx_attn_call(sink_perm, qa_c, ka_c, va_c, qb_c, kb_c, vb_c)
            xc = _tail_call(ya_c, yb_c, cu_c, bg_c, xc, mod_c, g_ffn_l, conv_c_l, w_o_l, wu, taps,
                            wd, c_len)
    return xl
```
